```python
import math
import jax
import jax.numpy as jnp
from jax import lax
import numpy as np

D_MODEL = 1024
BATCH = 8
SEQ = 8192
DEPTH = 1
DEC_BATCH = 8
DEC_SEQ = 4096
PAST_LEN = 128

DELTA_HEAD_DIM = 128
N_DELTA_HEADS = D_MODEL // DELTA_HEAD_DIM
D_DELTA = N_DELTA_HEADS * DELTA_HEAD_DIM
CONV_WIDTH = 5
CHUNK = 64
POOL_WINDOWS = (2, 4, 8, 16)
N_POOL_GROUPS = len(POOL_WINDOWS)
POOL_GROUP_DIM = D_MODEL // 8
D_POOL = N_POOL_GROUPS * POOL_GROUP_DIM
D_FF = ((8 * D_MODEL // 3 + 255) // 256) * 256
N_DIRS = 2
RMS_EPS = 1e-6
L2_EPS = 1e-6
_COL_SIZES = (3 * D_DELTA, D_DELTA, N_DIRS * N_DELTA_HEADS, N_DIRS * N_DELTA_HEADS, D_POOL, 2 * D_MODEL)
D_IN_PROJ = sum(_COL_SIZES)
SPLIT_POINTS = tuple(int(s) for s in np.cumsum(_COL_SIZES)[:-1])

kernel_name = 'hybrid_bidir_deltanet_pool_encoder'


def rms_norm(x, g):
    xf = x.astype(jnp.float32)
    y = xf * lax.rsqrt(jnp.mean(xf * xf, axis=-1, keepdims=True) + RMS_EPS)
    return (y * g.astype(jnp.float32)).astype(x.dtype)


def l2norm(x):
    return x * lax.rsqrt(jnp.sum(x * x, axis=-1, keepdims=True) + L2_EPS)


def swiglu_ffn(x, w_in, w_out):
    gate, up = jnp.split(x @ w_in, 2, axis=-1)
    return (jax.nn.silu(gate) * up) @ w_out


def centred_depthwise_conv(x, w):
    c = x.shape[-1]
    pad = CONV_WIDTH // 2
    return lax.conv_general_dilated(
        x, w[:, None, :].astype(x.dtype), window_strides=(1,), padding=[(pad, pad)],
        dimension_numbers=('NWC', 'WIO', 'NWC'), feature_group_count=c)


def gated_delta_chunked(q, k, v, beta, g):
    b, l, h, dk = q.shape
    dv = v.shape[-1]
    n = l // CHUNK

    def to_chunks(t):
        t = t.reshape((b, n, CHUNK, h) + t.shape[3:])
        return jnp.moveaxis(t, (1, 3), (0, 2))

    qc, kc, vc, bc, gc = (to_chunks(t) for t in (q, k, v, beta, g))
    gc = jnp.cumsum(gc, axis=-1)
    idx = jnp.arange(CHUNK)
    strict = idx[:, None] > idx[None, :]
    lower = idx[:, None] >= idx[None, :]
    decay = jnp.exp(jnp.where(lower, gc[..., :, None] - gc[..., None, :], -jnp.inf))
    kb = kc * bc[..., None]
    vb = vc * bc[..., None]
    lmat = jnp.where(strict, jnp.einsum('nbhid,nbhjd->nbhij', kb, kc) * decay, 0.0)
    amat = lmat + jnp.eye(CHUNK, dtype=lmat.dtype)
    rhs = jnp.concatenate([vb, kb * jnp.exp(gc)[..., None]], axis=-1)
    sol = lax.linalg.triangular_solve(amat, rhs, left_side=True, lower=True, unit_diagonal=True)
    u_c, w_c = sol[..., :dv], sol[..., dv:]
    qk = jnp.where(lower, jnp.einsum('nbhid,nbhjd->nbhij', qc, kc) * decay, 0.0)

    def step(s, inp):
        q_i, k_i, u_i, w_i, qk_i, g_i = inp
        v_new = u_i - jnp.einsum('bhck,bhkv->bhcv', w_i, s)
        o = (jnp.einsum('bhck,bhkv->bhcv', q_i * jnp.exp(g_i)[..., None], s)
             + jnp.einsum('bhij,bhjv->bhiv', qk_i, v_new))
        g_last = g_i[..., -1]
        s = (s * jnp.exp(g_last)[..., None, None]
             + jnp.einsum('bhck,bhcv->bhkv', k_i * jnp.exp(g_last[..., None] - g_i)[..., None], v_new))
        return s, o

    s0 = jnp.zeros((b, h, dk, dv), jnp.float32)
    _, o = lax.scan(step, s0, (qc, kc, u_c, w_c, qk, gc))
    return jnp.moveaxis(o, (0, 2), (1, 3)).reshape(b, l, h, dv)


def multiscale_pool(p):
    b, l, _ = p.shape
    pf = p.astype(jnp.float32).reshape(b, l, N_POOL_GROUPS, POOL_GROUP_DIM)
    csum = jnp.concatenate(
        [jnp.zeros((b, 1, N_POOL_GROUPS, POOL_GROUP_DIM), jnp.float32), jnp.cumsum(pf, axis=1)], axis=1)
    pos = np.arange(l)
    outs = []
    for gi, w in enumerate(POOL_WINDOWS):
        lo = np.clip(pos - w // 2, 0, l)
        hi = np.clip(pos + w - w // 2, 0, l)
        cg = csum[:, :, gi]
        cnt = jnp.asarray(hi - lo, jnp.float32)[None, :, None]
        outs.append((cg[:, hi] - cg[:, lo]) / cnt - pf[:, :, gi])
    return jnp.stack(outs, axis=2)


def token_mixer(u, w_in, conv_w, a_log, dt_bias, delta_norm_g, w_delta_out,
                w_pool_mix, pool_scale, w_pool_out, w_o):
    b, l, _ = u.shape
    h, dh = N_DELTA_HEADS, DELTA_HEAD_DIM
    qkv, z, dec, bet, pool_in, gates = jnp.split(u @ w_in, SPLIT_POINTS, axis=-1)
    qkv = jax.nn.silu(centred_depthwise_conv(qkv, conv_w)).astype(jnp.float32)
    qkv = qkv.reshape(b, l, 3, h, dh)
    q = l2norm(qkv[:, :, 0]) * (dh ** -0.5)
    k = l2norm(qkv[:, :, 1])
    v = qkv[:, :, 2]
    dec = dec.astype(jnp.float32).reshape(b, l, N_DIRS, h)
    g = -jnp.exp(a_log.astype(jnp.float32)) * jax.nn.softplus(dec + dt_bias.astype(jnp.float32))
    beta = jax.nn.sigmoid(bet.astype(jnp.float32)).reshape(b, l, N_DIRS, h)
    o_fwd = gated_delta_chunked(q, k, v, beta[:, :, 0], g[:, :, 0])
    rev = lambda t: jnp.flip(t, axis=1)
    o_bwd = rev(gated_delta_chunked(rev(q), rev(k), rev(v), rev(beta[:, :, 1]), rev(g[:, :, 1])))
    o = rms_norm(o_fwd + o_bwd, delta_norm_g)
    o = o * jax.nn.silu(z.astype(jnp.float32).reshape(b, l, h, dh))
    y_delta = o.astype(u.dtype).reshape(b, l, D_DELTA)
    pooled = multiscale_pool(pool_in)
    y_pool = jnp.einsum('blgc,gcd->blgd', pooled, w_pool_mix.astype(jnp.float32))
    y_pool = (y_pool * pool_scale.astype(jnp.float32)).astype(u.dtype).reshape(b, l, D_POOL)
    g_delta, g_pool = jnp.split(jax.nn.sigmoid(gates), 2, axis=-1)
    merged = g_delta * (y_delta @ w_delta_out) + g_pool * (y_pool @ w_pool_out)
    return merged @ w_o


def encoder_layer(h, ffn1_pre_g, ffn1_w_in, ffn1_w_out, ffn1_post_g, mix_pre_g, w_in, conv_w,
                  a_log, dt_bias, delta_norm_g, w_delta_out, w_pool_mix, pool_scale, w_pool_out,
                  w_o, mix_post_g, ffn2_pre_g, ffn2_w_in, ffn2_w_out, ffn2_post_g, final_g):
    h = h + 0.5 * rms_norm(swiglu_ffn(rms_norm(h, ffn1_pre_g), ffn1_w_in, ffn1_w_out), ffn1_post_g)
    mix = token_mixer(rms_norm(h, mix_pre_g), w_in, conv_w, a_log, dt_bias, delta_norm_g,
                      w_delta_out, w_pool_mix, pool_scale, w_pool_out, w_o)
    h = h + rms_norm(mix, mix_post_g)
    h = h + 0.5 * rms_norm(swiglu_ffn(rms_norm(h, ffn2_pre_g), ffn2_w_in, ffn2_w_out), ffn2_post_g)
    return rms_norm(h, final_g)


def setup_inputs(seed: int = 0) -> dict:
    key = jax.random.key(seed)
    ks = jax.random.split(key, 32)
    f32 = jnp.float32
    L_ = DEPTH

    def dense(k, fan_in, shape):
        return jax.random.normal(k, shape, f32) * fan_in ** -0.5

    def gain(k, shape):
        return 1.0 + 0.1 * jax.random.normal(k, shape, f32)

    dt = jnp.exp(jax.random.uniform(ks[8], (L_, N_DIRS, N_DELTA_HEADS), f32,
                                    math.log(1e-3), math.log(1e-1)))
    return {
        'x_prompt': jax.random.normal(ks[0], (BATCH, SEQ, D_MODEL), f32),
        'x_sample': jax.random.normal(ks[1], (DEC_BATCH, DEC_SEQ, D_MODEL), f32),
        'ffn1_pre_g': gain(ks[2], (L_, D_MODEL)),
        'ffn1_w_in': dense(ks[3], D_MODEL, (L_, D_MODEL, 2 * D_FF)),
        'ffn1_w_out': dense(ks[4], D_FF, (L_, D_FF, D_MODEL)),
        'ffn1_post_g': gain(ks[5], (L_, D_MODEL)),
        'mix_pre_g': gain(ks[6], (L_, D_MODEL)),
        'w_in': dense(ks[7], D_MODEL, (L_, D_MODEL, D_IN_PROJ)),
        'conv_w': dense(ks[9], CONV_WIDTH, (L_, CONV_WIDTH, 3 * D_DELTA)),
        'a_log': jnp.log(jax.random.uniform(ks[10], (L_, N_DIRS, N_DELTA_HEADS), f32, 1.0, 16.0)),
        'dt_bias': dt + jnp.log(-jnp.expm1(-dt)),
        'delta_norm_g': gain(ks[11], (L_, DELTA_HEAD_DIM)),
        'w_delta_out': dense(ks[12], D_DELTA, (L_, D_DELTA, D_MODEL)),
        'w_pool_mix': dense(ks[13], POOL_GROUP_DIM, (L_, N_POOL_GROUPS, POOL_GROUP_DIM, POOL_GROUP_DIM)),
        'pool_scale': gain(ks[14], (L_, N_POOL_GROUPS, POOL_GROUP_DIM)),
        'w_pool_out': dense(ks[15], D_POOL, (L_, D_POOL, D_MODEL)),
        'w_o': dense(ks[16], D_MODEL, (L_, D_MODEL, D_MODEL)),
        'mix_post_g': gain(ks[17], (L_, D_MODEL)),
        'ffn2_pre_g': gain(ks[18], (L_, D_MODEL)),
        'ffn2_w_in': dense(ks[19], D_MODEL, (L_, D_MODEL, 2 * D_FF)),
        'ffn2_w_out': dense(ks[20], D_FF, (L_, D_FF, D_MODEL)),
        'ffn2_post_g': gain(ks[21], (L_, D_MODEL)),
        'final_g': gain(ks[22], (L_, D_MODEL)),
    }


def reference(x_prompt, x_sample, ffn1_pre_g, ffn1_w_in, ffn1_w_out, ffn1_post_g, mix_pre_g,
              w_in, conv_w, a_log, dt_bias, delta_norm_g, w_delta_out, w_pool_mix, pool_scale,
              w_pool_out, w_o, mix_post_g, ffn2_pre_g, ffn2_w_in, ffn2_w_out, ffn2_post_g, final_g):
    weights = (ffn1_pre_g, ffn1_w_in, ffn1_w_out, ffn1_post_g, mix_pre_g, w_in, conv_w, a_log,
               dt_bias, delta_norm_g, w_delta_out, w_pool_mix, pool_scale, w_pool_out, w_o,
               mix_post_g, ffn2_pre_g, ffn2_w_in, ffn2_w_out, ffn2_post_g, final_g)

    def trunk(x):
        h = x
        for layer in range(DEPTH):
            h = encoder_layer(h, *[w[layer] for w in weights])
        return h

    y_prompt = trunk(x_prompt)
    y_sample = trunk(x_sample)
    return (y_prompt, y_sample)
```

```python
import functools

import jax
import jax.numpy as jnp
from jax import lax
from jax.experimental import pallas as pl
from jax.experimental.pallas import tpu as pltpu

F32 = jnp.float32
BF16 = jnp.bfloat16

D_MODEL = 1024
HEAD_DIM = 128
N_HEADS = D_MODEL // HEAD_DIM
N_DIRS = 2
CONV_WIDTH = 5
POOL_WINDOWS = (2, 4, 8, 16)
POOL_GROUP_DIM = 128
D_POOL = len(POOL_WINDOWS) * POOL_GROUP_DIM
D_FF = 2816
RMS_EPS = 1e-6
L2_EPS = 1e-6

LANES = 128
FF_CHUNK = 256
TOKEN_TILE = 512
DELTA_CHUNK = 128
DELTA_CHUNKS_PER_STEP = 2
QKV_HALO = 16
POOL_HALO = 8
VMEM_LIMIT_BYTES = 56 * 1024 * 1024


def _rms(x, g):
    ms = jnp.mean(x * x, axis=-1, keepdims=True)
    return x * lax.rsqrt(ms + RMS_EPS) * g


def _sigmoid(x):
    return 1.0 / (1.0 + jnp.exp(-x))


def _silu(x):
    return x * _sigmoid(x)


def _dot(a, b):
    return jnp.dot(a, b, preferred_element_type=F32)


def _resident(shape):
    return pl.BlockSpec(shape, lambda *_: (0,) * len(shape), pipeline_mode=pl.Buffered(1))


def _params(n_grid_dims):
    return pltpu.CompilerParams(
        dimension_semantics=("arbitrary",) * n_grid_dims,
        vmem_limit_bytes=VMEM_LIMIT_BYTES)


def _ffn_body(x_ref, pre_g_ref, w_in_ref, w_out_ref, post_g_ref, final_g_ref, o_ref, *, apply_final):
    x = x_ref[...]
    xn = _rms(x, pre_g_ref[...]).astype(BF16)
    acc = jnp.zeros(x.shape, F32)
    for c in range(D_FF // FF_CHUNK):
        lo = c * FF_CHUNK
        gate = _dot(xn, w_in_ref[:, lo:lo + FF_CHUNK])
        up = _dot(xn, w_in_ref[:, D_FF + lo:D_FF + lo + FF_CHUNK])
        act = (_silu(gate) * up).astype(BF16)
        acc = acc + _dot(act, w_out_ref[lo:lo + FF_CHUNK, :])
    h = x + 0.5 * _rms(acc, post_g_ref[...])
    if apply_final:
        h = _rms(h, final_g_ref[...])
    o_ref[...] = h


def _ffn(x, pre_g, w_in, w_out, post_g, final_g, *, apply_final):
    t = x.shape[0]
    tm = TOKEN_TILE
    row = pl.BlockSpec((tm, D_MODEL), lambda i: (i, 0))
    return pl.pallas_call(
        functools.partial(_ffn_body, apply_final=apply_final),
        grid=(t // tm,),
        in_specs=[row, _resident((1, D_MODEL)), _resident((D_MODEL, 2 * D_FF)),
                  _resident((D_FF, D_MODEL)), _resident((1, D_MODEL)), _resident((1, D_MODEL))],
        out_specs=row,
        out_shape=jax.ShapeDtypeStruct((t, D_MODEL), F32),
        compiler_params=_params(1),
        name="ffn_final" if apply_final else "ffn",
    )(x, pre_g, w_in, w_out, post_g, final_g)


def _qkv_body(xp_ref, x_ref, xn_ref, g_ref, w_ref, cw_ref, q_ref, k_ref, v_ref, xe_scr, raw_scr,
              *, tm, tiles_per_seq):
    t = lax.rem(pl.program_id(0), tiles_per_seq)
    g = g_ref[...]
    keep_prev = (t > 0).astype(F32)
    keep_next = (t < tiles_per_seq - 1).astype(F32)
    xe_scr[0:QKV_HALO, :] = (_rms(xp_ref[...], g) * keep_prev).astype(BF16)
    xe_scr[QKV_HALO:QKV_HALO + tm, :] = _rms(x_ref[...], g).astype(BF16)
    xe_scr[QKV_HALO + tm:, :] = (_rms(xn_ref[...], g) * keep_next).astype(BF16)
    pad = CONV_WIDTH // 2
    for s, out_ref in enumerate((q_ref, k_ref, v_ref)):
        cols = slice(s * D_MODEL, (s + 1) * D_MODEL)
        raw_scr[...] = _dot(xe_scr[...], w_ref[:, cols])
        acc = None
        for i in range(CONV_WIDTH):
            tap = raw_scr[pl.ds(QKV_HALO - pad + i, tm), :] * cw_ref[i:i + 1, cols]
            acc = tap if acc is None else acc + tap
        y = _silu(acc)
        if s == 2:
            out_ref[...] = y
            continue
        scale = HEAD_DIM ** -0.5 if s == 0 else 1.0
        for h in range(N_HEADS):
            hs = slice(h * HEAD_DIM, (h + 1) * HEAD_DIM)
            yh = y[:, hs]
            ss = jnp.sum(yh * yh, axis=-1, keepdims=True)
            out_ref[:, hs] = yh * (lax.rsqrt(ss + L2_EPS) * scale)


def _qkv(h, seq_len, g, w_qkv, conv_w):
    t = h.shape[0]
    tm = TOKEN_TILE
    tps = seq_len // tm
    halo_per_tile = tm // QKV_HALO
    n_halo_blocks = t // QKV_HALO
    row = pl.BlockSpec((tm, D_MODEL), lambda i: (i, 0))
    prev = pl.BlockSpec((QKV_HALO, D_MODEL), lambda i: (jnp.maximum(i * halo_per_tile - 1, 0), 0))
    nxt = pl.BlockSpec((QKV_HALO, D_MODEL),
                       lambda i: (jnp.minimum((i + 1) * halo_per_tile, n_halo_blocks - 1), 0))
    out = jax.ShapeDtypeStruct((t, D_MODEL), F32)
    return pl.pallas_call(
        functools.partial(_qkv_body, tm=tm, tiles_per_seq=tps),
        grid=(t // tm,),
        in_specs=[prev, row, nxt, _resident((1, D_MODEL)), _resident((D_MODEL, 3 * D_MODEL)),
                  _resident((CONV_WIDTH, 3 * D_MODEL))],
        out_specs=[row, row, row],
        out_shape=[out, out, out],
        scratch_shapes=[pltpu.VMEM((tm + 2 * QKV_HALO, D_MODEL), BF16),
                        pltpu.VMEM((tm + 2 * QKV_HALO, D_MODEL), F32)],
        compiler_params=_params(1),
        name="qkv_conv",
    )(h, h, h, g, w_qkv, conv_w)


def _misc_body(x_ref, g_ref, wz_ref, wgb_ref, wp_ref, wg_ref, alog_ref, dtb_ref,
               sz_ref, gb_ref, p_ref, gt_ref):
    xn = _rms(x_ref[...], g_ref[...]).astype(BF16)
    sz_ref[...] = _silu(_dot(xn, wz_ref[...]))
    raw = _dot(xn, wgb_ref[...])
    sp_in = raw + dtb_ref[...]
    softplus = jnp.maximum(sp_in, 0.0) + jnp.log(1.0 + jnp.exp(-jnp.abs(sp_in)))
    log_decay = -jnp.exp(alog_ref[...]) * softplus
    lane = lax.broadcasted_iota(jnp.int32, raw.shape, 1)
    gb_ref[...] = jnp.where(lane < N_DIRS * N_HEADS, log_decay, _sigmoid(raw))
    p_ref[...] = _dot(xn, wp_ref[...])
    gt_ref[...] = _sigmoid(_dot(xn, wg_ref[...]))


def _misc(h, g, w_z, w_gb, w_pool, w_gates, alog_row, dtb_row):
    t = h.shape[0]
    tm = TOKEN_TILE

    def row(n):
        return pl.BlockSpec((tm, n), lambda i: (i, 0))

    def out(n):
        return jax.ShapeDtypeStruct((t, n), F32)

    return pl.pallas_call(
        _misc_body,
        grid=(t // tm,),
        in_specs=[row(D_MODEL), _resident((1, D_MODEL)), _resident((D_MODEL, D_MODEL)),
                  _resident((D_MODEL, LANES)), _resident((D_MODEL, D_POOL)),
                  _resident((D_MODEL, 2 * D_MODEL)), _resident((1, LANES)), _resident((1, LANES))],
        out_specs=[row(D_MODEL), row(LANES), row(D_POOL), row(2 * D_MODEL)],
        out_shape=[out(D_MODEL), out(LANES), out(D_POOL), out(2 * D_MODEL)],
        compiler_params=_params(1),
        name="misc_proj",
    )(h, g, w_z, w_gb, w_pool, w_gates, alog_row, dtb_row)


def _split3(x):
    a = x.astype(BF16)
    r = x - a.astype(F32)
    b = r.astype(BF16)
    c = (r - b.astype(F32)).astype(BF16)
    return a, b, c


def _delta_body(q_ref, k_ref, v_ref, gb_ref, *rest, reverse, n_chunks):
    if reverse:
        ofwd_ref, sz_ref, ng_ref, out_ref, s_scr = rest
    else:
        out_ref, s_scr = rest
    c = DELTA_CHUNK

    @pl.when(pl.program_id(1) == 0)
    def _():
        s_scr[...] = jnp.zeros(s_scr.shape, F32)

    row = lax.broadcasted_iota(jnp.int32, (c, c), 0)
    col = lax.broadcasted_iota(jnp.int32, (c, c), 1)
    incl = (row <= col) if reverse else (row >= col)
    strict = (row < col) if reverse else (row > col)
    last = 0 if reverse else c - 1
    direction = 1 if reverse else 0
    ones_tri = incl.astype(BF16)
    n_steps = c.bit_length() - 2
    heads = range(N_HEADS)
    pairs = [(g, h) for g in range(n_chunks) for h in heads]

    gbs, gcs, gcts = [], [], []
    for g in range(n_chunks):
        gb = gb_ref[g * c:(g + 1) * c, :]
        gc = sum(_dot(ones_tri, part) for part in _split3(gb))
        gbs.append(gb)
        gcs.append(gc)
        gcts.append(gc.T)

    loc = {}
    for g, h in pairs:
        rows = slice(g * c, (g + 1) * c)
        hs = slice(h * HEAD_DIM, (h + 1) * HEAD_DIM)
        cg = direction * N_HEADS + h
        cb = N_DIRS * N_HEADS + cg
        gc_c = gcs[g][:, cg:cg + 1]
        gc_r = gcts[g][cg:cg + 1, :]
        beta_c = gbs[g][:, cb:cb + 1]
        decay = jnp.exp(jnp.where(incl, gc_c - gc_r, -jnp.inf))
        e_c = jnp.exp(gc_c)
        q = q_ref[rows, hs]
        k = k_ref[rows, hs]
        v = v_ref[rows, hs]
        kt = k.T
        kb = k * beta_c
        pr = _dot(jnp.concatenate([kb, q], axis=0).astype(BF16), kt.astype(BF16))
        lmat = jnp.where(strict, pr[:c] * decay, 0.0)
        loc[g, h] = dict(
            lmat=lmat, qk=(pr[c:] * decay).astype(BF16),
            rhs=jnp.concatenate([v * beta_c, kb * e_c], axis=1),
            qe=(q * e_c).astype(BF16),
            k_decay_t=(kt * decay[last:last + 1, :]).astype(BF16),
            state_decay=jnp.exp(gcts[g][cg:cg + 1, last:last + 1]))

    for p in pairs:
        lb = loc[p]["lmat"].astype(BF16)
        loc[p]["m"] = _dot(lb, lb)
        loc[p]["r"] = -loc[p].pop("lmat")
    for step in range(1, n_steps + 1):
        for p in pairs:
            r, m = loc[p]["r"], loc[p]["m"]
            mb = m.astype(BF16)
            if step < n_steps:
                pr2 = _dot(jnp.concatenate([r, m], axis=0).astype(BF16), mb)
                loc[p]["r"] = r + m + pr2[:c]
                loc[p]["m"] = pr2[c:]
            else:
                loc[p]["r"] = r + m + _dot(r.astype(BF16), mb)
    for p in pairs:
        rhs = loc[p].pop("rhs")
        sol = rhs + _dot(loc[p].pop("r").astype(BF16), rhs.astype(BF16))
        loc[p]["u"] = sol[:, :HEAD_DIM]
        loc[p]["w"] = sol[:, HEAD_DIM:].astype(BF16)

    states = [s_scr[h] for h in heads]
    for g in (reversed(range(n_chunks)) if reverse else range(n_chunks)):
        rows = slice(g * c, (g + 1) * c)
        pr3 = [_dot(jnp.concatenate([loc[g, h]["w"], loc[g, h]["qe"]], axis=0), states[h].astype(BF16))
               for h in heads]
        v_new = [(loc[g, h]["u"] - pr3[h][:c]).astype(BF16) for h in heads]
        states = [states[h] * loc[g, h]["state_decay"] + _dot(loc[g, h]["k_decay_t"], v_new[h])
                  for h in heads]
        for h in heads:
            hs = slice(h * HEAD_DIM, (h + 1) * HEAD_DIM)
            o = pr3[h][c:] + _dot(loc[g, h]["qk"], v_new[h])
            if reverse:
                tot = o + ofwd_ref[rows, hs]
                ms = jnp.mean(tot * tot, axis=-1, keepdims=True)
                y = tot * lax.rsqrt(ms + RMS_EPS) * ng_ref[...] * sz_ref[rows, hs]
                out_ref[rows, hs] = y.astype(out_ref.dtype)
            else:
                out_ref[rows, hs] = o
    for h in heads:
        s_scr[h] = states[h]


def _delta(q, k, v, gb, batch, seq_len, merge=None):
    reverse = merge is not None
    t = q.shape[0]
    c = DELTA_CHUNKS_PER_STEP * DELTA_CHUNK
    n = seq_len // c

    def blk(b, j):
        return (b * n + (n - 1 - j if reverse else j), 0)

    def row(width):
        return pl.BlockSpec((c, width), blk)

    in_specs = [row(D_MODEL), row(D_MODEL), row(D_MODEL), row(LANES)]
    operands = [q, k, v, gb]
    if reverse:
        in_specs += [row(D_MODEL), row(D_MODEL), _resident((1, HEAD_DIM))]
        operands += list(merge)
    return pl.pallas_call(
        functools.partial(_delta_body, reverse=reverse, n_chunks=DELTA_CHUNKS_PER_STEP),
        grid=(batch, n),
        in_specs=in_specs,
        out_specs=row(D_MODEL),
        out_shape=jax.ShapeDtypeStruct((t, D_MODEL), BF16 if reverse else F32),
        scratch_shapes=[pltpu.VMEM((N_HEADS, HEAD_DIM, HEAD_DIM), F32)],
        compiler_params=_params(2),
        name="delta_bwd" if reverse else "delta_fwd",
    )(*operands)


def _mix_out_body(h_ref, yd_ref, gt_ref, pp_ref, p_ref, pn_ref, wdo_ref, wpm_ref, ps_ref, wpo_ref,
                  wo_ref, g_ref, o_ref, pad_scr, *, tm, tiles_per_seq, seq_len):
    t = lax.rem(pl.program_id(0), tiles_per_seq)
    keep_prev = (t > 0).astype(F32)
    keep_next = (t < tiles_per_seq - 1).astype(F32)
    pad_scr[0:POOL_HALO, :] = pp_ref[...] * keep_prev
    pad_scr[POOL_HALO:POOL_HALO + tm, :] = p_ref[...]
    pad_scr[POOL_HALO + tm:, :] = pn_ref[...] * keep_next
    pos = t * tm + lax.broadcasted_iota(jnp.int32, (tm, 1), 0)
    pooled_parts = []
    for gi, win in enumerate(POOL_WINDOWS):
        cols = slice(gi * POOL_GROUP_DIM, (gi + 1) * POOL_GROUP_DIM)
        back = win // 2
        fwd = win - back
        total = None
        for off in range(-back, fwd):
            part = pad_scr[pl.ds(POOL_HALO + off, tm), cols]
            total = part if total is None else total + part
        cnt = (jnp.minimum(pos + fwd, seq_len) - jnp.maximum(pos - back, 0)).astype(F32)
        pooled = total / cnt - p_ref[:, cols]
        yp = _dot(pooled.astype(BF16), wpm_ref[gi]) * ps_ref[:, cols]
        pooled_parts.append(yp.astype(BF16))
    y_pool = jnp.concatenate(pooled_parts, axis=1)
    a = _dot(yd_ref[...], wdo_ref[...])
    b = _dot(y_pool, wpo_ref[...])
    merged = gt_ref[:, :D_MODEL] * a + gt_ref[:, D_MODEL:] * b
    mix = _dot(merged.astype(BF16), wo_ref[...])
    o_ref[...] = h_ref[...] + _rms(mix, g_ref[...])


def _mix_out(h, y_delta, gates, pool_in, seq_len, w_delta_out, w_pool_mix, pool_scale, w_pool_out,
             w_o, post_g):
    t = h.shape[0]
    tm = TOKEN_TILE
    tps = seq_len // tm
    halo_per_tile = tm // POOL_HALO
    n_halo_blocks = t // POOL_HALO

    def row(n):
        return pl.BlockSpec((tm, n), lambda i: (i, 0))

    prev = pl.BlockSpec((POOL_HALO, D_POOL), lambda i: (jnp.maximum(i * halo_per_tile - 1, 0), 0))
    nxt = pl.BlockSpec((POOL_HALO, D_POOL),
                       lambda i: (jnp.minimum((i + 1) * halo_per_tile, n_halo_blocks - 1), 0))
    return pl.pallas_call(
        functools.partial(_mix_out_body, tm=tm, tiles_per_seq=tps, seq_len=seq_len),
        grid=(t // tm,),
        in_specs=[row(D_MODEL), row(D_MODEL), row(2 * D_MODEL), prev, row(D_POOL), nxt,
                  _resident((D_MODEL, D_MODEL)),
                  _resident((len(POOL_WINDOWS), POOL_GROUP_DIM, POOL_GROUP_DIM)),
                  _resident((1, D_POOL)), _resident((D_POOL, D_MODEL)),
                  _resident((D_MODEL, D_MODEL)), _resident((1, D_MODEL))],
        out_specs=row(D_MODEL),
        out_shape=jax.ShapeDtypeStruct((t, D_MODEL), F32),
        scratch_shapes=[pltpu.VMEM((tm + 2 * POOL_HALO, D_POOL), F32)],
        compiler_params=_params(1),
        name="mix_out",
    )(h, y_delta, gates, pool_in, pool_in, pool_in, w_delta_out, w_pool_mix, pool_scale, w_pool_out,
      w_o, post_g)


def _layer(x, w):
    batch, seq_len, _ = x.shape
    assert seq_len % TOKEN_TILE == 0 and seq_len % (DELTA_CHUNKS_PER_STEP * DELTA_CHUNK) == 0
    h0 = x.reshape(batch * seq_len, D_MODEL)
    h1 = _ffn(h0, w["ffn1_pre_g"], w["ffn1_w_in"], w["ffn1_w_out"], w["ffn1_post_g"], w["final_g"],
              apply_final=False)
    q, k, v = _qkv(h1, seq_len, w["mix_pre_g"], w["w_qkv"], w["conv_w"])
    sz, gb, pool_in, gates = _misc(h1, w["mix_pre_g"], w["w_z"], w["w_gb"], w["w_pool"], w["w_gates"],
                                   w["alog_row"], w["dtb_row"])
    o_fwd = _delta(q, k, v, gb, batch, seq_len)
    y_delta = _delta(q, k, v, gb, batch, seq_len, merge=(o_fwd, sz, w["delta_norm_g"]))
    h2 = _mix_out(h1, y_delta, gates, pool_in, seq_len, w["w_delta_out"], w["w_pool_mix"],
                  w["pool_scale"], w["w_pool_out"], w["w_o"], w["mix_post_g"])
    h3 = _ffn(h2, w["ffn2_pre_g"], w["ffn2_w_in"], w["ffn2_w_out"], w["ffn2_post_g"], w["final_g"],
              apply_final=True)
    return h3.reshape(batch, seq_len, D_MODEL)


def _prepare_weights(ffn1_pre_g, ffn1_w_in, ffn1_w_out, ffn1_post_g, mix_pre_g, w_in, conv_w, a_log,
                     dt_bias, delta_norm_g, w_delta_out, w_pool_mix, pool_scale, w_pool_out, w_o,
                     mix_post_g, ffn2_pre_g, ffn2_w_in, ffn2_w_out, ffn2_post_g, final_g):
    def gain(g):
        return g[0].reshape(1, -1).astype(F32)

    def mxu(m):
        return m.astype(BF16)

    n_dh = N_DIRS * N_HEADS
    o_z = 3 * D_MODEL
    o_dec = o_z + D_MODEL
    o_pool = o_dec + 2 * n_dh
    o_gates = o_pool + D_POOL
    wi = w_in[0]
    lane_pad = LANES - 2 * n_dh
    w_gb = jnp.pad(wi[:, o_dec:o_pool], ((0, 0), (0, lane_pad)))
    alog_row = jnp.pad(a_log[0].reshape(1, n_dh), ((0, 0), (0, LANES - n_dh))).astype(F32)
    dtb_row = jnp.pad(dt_bias[0].reshape(1, n_dh), ((0, 0), (0, LANES - n_dh))).astype(F32)
    return dict(
        ffn1_pre_g=gain(ffn1_pre_g), ffn1_w_in=mxu(ffn1_w_in[0]), ffn1_w_out=mxu(ffn1_w_out[0]),
        ffn1_post_g=gain(ffn1_post_g), mix_pre_g=gain(mix_pre_g),
        w_qkv=mxu(wi[:, :o_z]), w_z=mxu(wi[:, o_z:o_dec]), w_gb=mxu(w_gb),
        w_pool=mxu(wi[:, o_pool:o_gates]), w_gates=mxu(wi[:, o_gates:]),
        conv_w=conv_w[0].astype(F32), alog_row=alog_row, dtb_row=dtb_row,
        delta_norm_g=gain(delta_norm_g), w_delta_out=mxu(w_delta_out[0]),
        w_pool_mix=mxu(w_pool_mix[0]), pool_scale=pool_scale[0].reshape(1, D_POOL).astype(F32),
        w_pool_out=mxu(w_pool_out[0]), w_o=mxu(w_o[0]), mix_post_g=gain(mix_post_g),
        ffn2_pre_g=gain(ffn2_pre_g), ffn2_w_in=mxu(ffn2_w_in[0]), ffn2_w_out=mxu(ffn2_w_out[0]),
        ffn2_post_g=gain(ffn2_post_g), final_g=gain(final_g))


def kernel(x_prompt, x_sample, ffn1_pre_g, ffn1_w_in, ffn1_w_out, ffn1_post_g, mix_pre_g, w_in, conv_w,
           a_log, dt_bias, delta_norm_g, w_delta_out, w_pool_mix, pool_scale, w_pool_out, w_o,
           mix_post_g, ffn2_pre_g, ffn2_w_in, ffn2_w_out, ffn2_post_g, final_g):
    assert ffn1_w_in.shape[0] == 1, "single-layer trunk"
    w = _prepare_weights(ffn1_pre_g, ffn1_w_in, ffn1_w_out, ffn1_post_g, mix_pre_g, w_in, conv_w, a_log,
                         dt_bias, delta_norm_g, w_delta_out, w_pool_mix, pool_scale, w_pool_out, w_o,
                         mix_post_g, ffn2_pre_g, ffn2_w_in, ffn2_w_out, ffn2_post_g, final_g)
    return (_layer(x_prompt, w), _layer(x_sample, w))
```

```python
import functools

import jax
import jax.numpy as jnp
from jax import lax
from jax.experimental import pallas as pl
from jax.experimental.pallas import tpu as pltpu

F32 = jnp.float32
BF16 = jnp.bfloat16

D_MODEL = 1024
HEAD_DIM = 128
N_HEADS = D_MODEL // HEAD_DIM
N_DIRS = 2
CONV_WIDTH = 5
POOL_WINDOWS = (2, 4, 8, 16)
POOL_GROUP_DIM = 128
D_POOL = len(POOL_WINDOWS) * POOL_GROUP_DIM
D_FF = 2816
RMS_EPS = 1e-6
L2_EPS = 1e-6

LANES = 128
SUBLANES = 8
CONV_ROW_STRIDE = 4
FF_CHUNK = 256
TOKEN_TILE = 512
DELTA_CHUNK = 128
DELTA_CHUNKS_PER_STEP = 4
QKV_HALO = 16
POOL_HALO = 8
VMEM_LIMIT_BYTES = 56 * 1024 * 1024


def _rms(x, g):
    ms = jnp.mean(x * x, axis=-1, keepdims=True)
    return x * lax.rsqrt(ms + RMS_EPS) * g


def _sigmoid(x):
    return 1.0 / (1.0 + jnp.exp(-x))


def _silu(x):
    return x * _sigmoid(x)


def _dot(a, b):
    return jnp.dot(a, b, preferred_element_type=F32)


def _resident(shape):
    return pl.BlockSpec(shape, lambda *_: (0,) * len(shape), pipeline_mode=pl.Buffered(1))


def _params(n_grid_dims):
    return pltpu.CompilerParams(
        dimension_semantics=("arbitrary",) * n_grid_dims,
        vmem_limit_bytes=VMEM_LIMIT_BYTES)


def _ffn_body(x_ref, pre_g_ref, w_in_ref, w_out_ref, post_g_ref, final_g_ref, o_ref, *, apply_final):
    x = x_ref[...]
    xn = _rms(x, pre_g_ref[...]).astype(BF16)
    acc = jnp.zeros(x.shape, F32)
    for c in range(D_FF // FF_CHUNK):
        lo = c * FF_CHUNK
        gate = _dot(xn, w_in_ref[:, lo:lo + FF_CHUNK])
        up = _dot(xn, w_in_ref[:, D_FF + lo:D_FF + lo + FF_CHUNK])
        act = (_silu(gate) * up).astype(BF16)
        acc = acc + _dot(act, w_out_ref[lo:lo + FF_CHUNK, :])
    h = x + 0.5 * _rms(acc, post_g_ref[...])
    if apply_final:
        h = _rms(h, final_g_ref[...])
    o_ref[...] = h


def _ffn(x, pre_g, w_in, w_out, post_g, final_g, *, apply_final):
    t = x.shape[0]
    tm = TOKEN_TILE
    row = pl.BlockSpec((tm, D_MODEL), lambda i: (i, 0))
    return pl.pallas_call(
        functools.partial(_ffn_body, apply_final=apply_final),
        grid=(t // tm,),
        in_specs=[row, _resident((1, D_MODEL)), _resident((D_MODEL, 2 * D_FF)),
                  _resident((D_FF, D_MODEL)), _resident((1, D_MODEL)), _resident((1, D_MODEL))],
        out_specs=row,
        out_shape=jax.ShapeDtypeStruct((t, D_MODEL), F32),
        compiler_params=_params(1),
        name="ffn_final" if apply_final else "ffn",
    )(x, pre_g, w_in, w_out, post_g, final_g)


def _qkv_body(xp_ref, x_ref, xn_ref, g_ref, w_ref, cw_ref, q_ref, k_ref, kt_ref, v_ref,
              xe_scr, raw_scr, stage_scr, *, tm, tiles_per_seq):
    t = lax.rem(pl.program_id(0), tiles_per_seq)
    g = g_ref[...]
    keep_prev = (t > 0).astype(F32)
    keep_next = (t < tiles_per_seq - 1).astype(F32)
    xe_scr[0:QKV_HALO, :] = (_rms(xp_ref[...], g) * keep_prev).astype(BF16)
    xe_scr[QKV_HALO:QKV_HALO + tm, :] = _rms(x_ref[...], g).astype(BF16)
    xe_scr[QKV_HALO + tm:, :] = (_rms(xn_ref[...], g) * keep_next).astype(BF16)
    pad = CONV_WIDTH // 2
    group = SUBLANES * CONV_ROW_STRIDE
    for s, out_ref in enumerate((q_ref, k_ref, v_ref)):
        raw = _dot(xe_scr[...], w_ref[:, s * D_MODEL:(s + 1) * D_MODEL])
        for h in range(N_HEADS):
            raw_scr[h] = raw[:, h * HEAD_DIM:(h + 1) * HEAD_DIM]
        scale = HEAD_DIM ** -0.5 if s == 0 else 1.0

        for h in range(N_HEADS):
            lanes = slice(s * D_MODEL + h * HEAD_DIM, s * D_MODEL + (h + 1) * HEAD_DIM)
            taps_w = [cw_ref[i:i + 1, lanes] for i in range(CONV_WIDTH)]
            for first in range(0, tm, group):
                for phase in range(CONV_ROW_STRIDE):
                    acc = None
                    for i in range(CONV_WIDTH):
                        start = QKV_HALO - pad + i + first + phase
                        tap = raw_scr[h, pl.ds(start, SUBLANES, stride=CONV_ROW_STRIDE), :] * taps_w[i]
                        acc = tap if acc is None else acc + tap
                    y = _silu(acc)
                    if s < 2:
                        ss = jnp.sum(y * y, axis=-1, keepdims=True)
                        y = y * (lax.rsqrt(ss + L2_EPS) * scale)
                    stage_scr[h, pl.ds(first + phase, SUBLANES, stride=CONV_ROW_STRIDE), :] = y
        for h in range(N_HEADS):
            hs = slice(h * HEAD_DIM, (h + 1) * HEAD_DIM)
            out_ref[:, hs] = stage_scr[h]
            if s == 1:
                kt_ref[hs, :] = stage_scr[h].T


def _qkv(h, seq_len, g, w_qkv, conv_w):
    t = h.shape[0]
    tm = TOKEN_TILE
    tps = seq_len // tm
    halo_per_tile = tm // QKV_HALO
    n_halo_blocks = t // QKV_HALO
    row = pl.BlockSpec((tm, D_MODEL), lambda i: (i, 0))
    prev = pl.BlockSpec((QKV_HALO, D_MODEL), lambda i: (jnp.maximum(i * halo_per_tile - 1, 0), 0))
    nxt = pl.BlockSpec((QKV_HALO, D_MODEL),
                       lambda i: (jnp.minimum((i + 1) * halo_per_tile, n_halo_blocks - 1), 0))
    out = jax.ShapeDtypeStruct((t, D_MODEL), F32)
    col = pl.BlockSpec((D_MODEL, tm), lambda i: (0, i))
    return pl.pallas_call(
        functools.partial(_qkv_body, tm=tm, tiles_per_seq=tps),
        grid=(t // tm,),
        in_specs=[prev, row, nxt, _resident((1, D_MODEL)), _resident((D_MODEL, 3 * D_MODEL)),
                  _resident((CONV_WIDTH, 3 * D_MODEL))],
        out_specs=[row, row, col, row],
        out_shape=[out, out, jax.ShapeDtypeStruct((D_MODEL, t), F32), out],
        scratch_shapes=[pltpu.VMEM((tm + 2 * QKV_HALO, D_MODEL), BF16),
                        pltpu.VMEM((N_HEADS, tm + 2 * QKV_HALO, HEAD_DIM), F32),
                        pltpu.VMEM((N_HEADS, tm, HEAD_DIM), F32)],
        compiler_params=_params(1),
        name="qkv_conv",
    )(h, h, h, g, w_qkv, conv_w)


def _split3(x):
    a = x.astype(BF16)
    r = x - a.astype(F32)
    b = r.astype(BF16)
    c = (r - b.astype(F32)).astype(BF16)
    return a, b, c


def _misc_body(x_ref, g_ref, wz_ref, wgb_ref, wp_ref, wg_ref, alog_ref, dtb_ref,
               sz_ref, gb_ref, gct_ref, p_ref, gt_ref):
    xn = _rms(x_ref[...], g_ref[...]).astype(BF16)
    sz_ref[...] = _silu(_dot(xn, wz_ref[...]))
    raw = _dot(xn, wgb_ref[...])
    sp_in = raw + dtb_ref[...]
    softplus = jnp.maximum(sp_in, 0.0) + jnp.log(1.0 + jnp.exp(-jnp.abs(sp_in)))
    log_decay = -jnp.exp(alog_ref[...]) * softplus
    beta = _sigmoid(raw)
    c = DELTA_CHUNK
    row = lax.broadcasted_iota(jnp.int32, (c, c), 0)
    col = lax.broadcasted_iota(jnp.int32, (c, c), 1)
    ones_lower = (row >= col).astype(BF16)
    ones_upper = (row <= col).astype(BF16)
    lane = lax.broadcasted_iota(jnp.int32, (c, LANES), 1)
    for j in range(raw.shape[0] // c):
        rows = slice(j * c, (j + 1) * c)
        g_fwd = jnp.where(lane < N_HEADS, log_decay[rows], 0.0)
        g_bwd = jnp.where((lane >= N_HEADS) & (lane < N_DIRS * N_HEADS), log_decay[rows], 0.0)
        gc = (sum(_dot(ones_lower, part) for part in _split3(g_fwd))
              + sum(_dot(ones_upper, part) for part in _split3(g_bwd)))
        gb_ref[rows, :] = jnp.where(lane < N_DIRS * N_HEADS, gc, beta[rows])
        gct_ref[:, rows] = gc.T
    p_ref[...] = _dot(xn, wp_ref[...])
    gt_ref[...] = _sigmoid(_dot(xn, wg_ref[...]))


def _misc(h, g, w_z, w_gb, w_pool, w_gates, alog_row, dtb_row):
    t = h.shape[0]
    tm = TOKEN_TILE

    def row(n):
        return pl.BlockSpec((tm, n), lambda i: (i, 0))

    def out(n):
        return jax.ShapeDtypeStruct((t, n), F32)

    return pl.pallas_call(
        _misc_body,
        grid=(t // tm,),
        in_specs=[row(D_MODEL), _resident((1, D_MODEL)), _resident((D_MODEL, D_MODEL)),
                  _resident((D_MODEL, LANES)), _resident((D_MODEL, D_POOL)),
                  _resident((D_MODEL, 2 * D_MODEL)), _resident((1, LANES)), _resident((1, LANES))],
        out_specs=[row(D_MODEL), row(LANES), pl.BlockSpec((LANES, tm), lambda i: (0, i)), row(D_POOL),
                   row(2 * D_MODEL)],
        out_shape=[out(D_MODEL), out(LANES), jax.ShapeDtypeStruct((LANES, t), F32), out(D_POOL),
                   out(2 * D_MODEL)],
        compiler_params=_params(1),
        name="misc_proj",
    )(h, g, w_z, w_gb, w_pool, w_gates, alog_row, dtb_row)


def _delta_body(q_ref, k_ref, kt_ref, v_ref, gb_ref, gct_ref, *rest, reverse, n_chunks):
    if reverse:
        ofwd_ref, sz_ref, ng_ref, out_ref, s_scr = rest
    else:
        out_ref, s_scr = rest
    c = DELTA_CHUNK

    @pl.when(pl.program_id(1) == 0)
    def _():
        s_scr[...] = jnp.zeros(s_scr.shape, F32)

    row = lax.broadcasted_iota(jnp.int32, (c, c), 0)
    col = lax.broadcasted_iota(jnp.int32, (c, c), 1)
    incl = (row <= col) if reverse else (row >= col)
    strict = (row < col) if reverse else (row > col)
    last = 0 if reverse else c - 1
    direction = 1 if reverse else 0
    n_steps = c.bit_length() - 2
    heads = range(N_HEADS)
    pairs = [(g, h) for g in range(n_chunks) for h in heads]

    loc = {}
    for g, h in pairs:
        rows = slice(g * c, (g + 1) * c)
        hs = slice(h * HEAD_DIM, (h + 1) * HEAD_DIM)
        cg = direction * N_HEADS + h
        cb = N_DIRS * N_HEADS + cg
        gc_c = gb_ref[rows, cg:cg + 1]
        gc_r = gct_ref[cg:cg + 1, rows]
        beta_c = gb_ref[rows, cb:cb + 1]
        decay = jnp.exp(jnp.where(incl, gc_c - gc_r, -jnp.inf))
        e_c = jnp.exp(gc_c)
        q = q_ref[rows, hs]
        k = k_ref[rows, hs]
        v = v_ref[rows, hs]
        kt = kt_ref[hs, rows]
        kb = k * beta_c
        pr = _dot(jnp.concatenate([kb, q], axis=0).astype(BF16), kt.astype(BF16))
        lmat = jnp.where(strict, pr[:c] * decay, 0.0)
        loc[g, h] = dict(
            lmat=lmat, qk=(pr[c:] * decay).astype(BF16),
            rhs=jnp.concatenate([v * beta_c, kb * e_c], axis=1),
            qe=(q * e_c).astype(BF16),
            k_decay_t=(kt * decay[last:last + 1, :]).astype(BF16),
            state_decay=jnp.exp(gc_r[:, last:last + 1]))

    for p in pairs:
        lb = loc[p]["lmat"].astype(BF16)
        loc[p]["m"] = _dot(lb, lb)
        loc[p]["r"] = -loc[p].pop("lmat")
    for step in range(1, n_steps + 1):
        for p in pairs:
            r, m = loc[p]["r"], loc[p]["m"]
            mb = m.astype(BF16)
            if step < n_steps:
                pr2 = _dot(jnp.concatenate([r, m], axis=0).astype(BF16), mb)
                loc[p]["r"] = r + m + pr2[:c]
                loc[p]["m"] = pr2[c:]
            else:
                loc[p]["r"] = r + m + _dot(r.astype(BF16), mb)
    for p in pairs:
        rhs = loc[p].pop("rhs")
        sol = rhs + _dot(loc[p].pop("r").astype(BF16), rhs.astype(BF16))
        loc[p]["u"] = sol[:, :HEAD_DIM]
        loc[p]["w"] = sol[:, HEAD_DIM:].astype(BF16)

    states = [s_scr[h] for h in heads]
    for g in (reversed(range(n_chunks)) if reverse else range(n_chunks)):
        rows = slice(g * c, (g + 1) * c)
        pr3 = [_dot(jnp.concatenate([loc[g, h]["w"], loc[g, h]["qe"]], axis=0), states[h].astype(BF16))
               for h in heads]
        v_new = [(loc[g, h]["u"] - pr3[h][:c]).astype(BF16) for h in heads]
        pr4 = [_dot(jnp.concatenate([loc[g, h]["qk"], loc[g, h]["k_decay_t"]], axis=0), v_new[h])
               for h in heads]
        states = [states[h] * loc[g, h]["state_decay"] + pr4[h][c:] for h in heads]
        for h in heads:
            hs = slice(h * HEAD_DIM, (h + 1) * HEAD_DIM)
            o = pr3[h][c:] + pr4[h][:c]
            if reverse:
                tot = o + ofwd_ref[rows, hs]
                ms = jnp.mean(tot * tot, axis=-1, keepdims=True)
                y = tot * lax.rsqrt(ms + RMS_EPS) * ng_ref[...] * sz_ref[rows, hs]
                out_ref[rows, hs] = y.astype(out_ref.dtype)
            else:
                out_ref[rows, hs] = o
    for h in heads:
        s_scr[h] = states[h]


def _delta(q, k, kt, v, gb, gct, batch, seq_len, merge=None):
    reverse = merge is not None
    t = q.shape[0]
    c = DELTA_CHUNKS_PER_STEP * DELTA_CHUNK
    n = seq_len // c

    def blk(b, j):
        return (b * n + (n - 1 - j if reverse else j), 0)

    def row(width):
        return pl.BlockSpec((c, width), blk)

    def col(height):
        return pl.BlockSpec((height, c), lambda b, j: blk(b, j)[::-1])

    in_specs = [row(D_MODEL), row(D_MODEL), col(D_MODEL), row(D_MODEL), row(LANES), col(LANES)]
    operands = [q, k, kt, v, gb, gct]
    if reverse:
        in_specs += [row(D_MODEL), row(D_MODEL), _resident((1, HEAD_DIM))]
        operands += list(merge)
    return pl.pallas_call(
        functools.partial(_delta_body, reverse=reverse, n_chunks=DELTA_CHUNKS_PER_STEP),
        grid=(batch, n),
        in_specs=in_specs,
        out_specs=row(D_MODEL),
        out_shape=jax.ShapeDtypeStruct((t, D_MODEL), BF16 if reverse else F32),
        scratch_shapes=[pltpu.VMEM((N_HEADS, HEAD_DIM, HEAD_DIM), F32)],
        compiler_params=_params(2),
        name="delta_bwd" if reverse else "delta_fwd",
    )(*operands)


def _mix_out_body(h_ref, yd_ref, gt_ref, pp_ref, p_ref, pn_ref, wdo_ref, wpm_ref, ps_ref, wpo_ref,
                  wo_ref, g_ref, o_ref, pad_scr, *, tm, tiles_per_seq, seq_len):
    t = lax.rem(pl.program_id(0), tiles_per_seq)
    keep_prev = (t > 0).astype(F32)
    keep_next = (t < tiles_per_seq - 1).astype(F32)
    pad_scr[0:POOL_HALO, :] = pp_ref[...] * keep_prev
    pad_scr[POOL_HALO:POOL_HALO + tm, :] = p_ref[...]
    pad_scr[POOL_HALO + tm:, :] = pn_ref[...] * keep_next
    pos = t * tm + lax.broadcasted_iota(jnp.int32, (tm, 1), 0)
    pooled_parts = []
    for gi, win in enumerate(POOL_WINDOWS):
        cols = slice(gi * POOL_GROUP_DIM, (gi + 1) * POOL_GROUP_DIM)
        back = win // 2
        fwd = win - back
        total = None
        for off in range(-back, fwd):
            part = pad_scr[pl.ds(POOL_HALO + off, tm), cols]
            total = part if total is None else total + part
        cnt = (jnp.minimum(pos + fwd, seq_len) - jnp.maximum(pos - back, 0)).astype(F32)
        pooled = total / cnt - p_ref[:, cols]
        yp = _dot(pooled.astype(BF16), wpm_ref[gi]) * ps_ref[:, cols]
        pooled_parts.append(yp.astype(BF16))
    y_pool = jnp.concatenate(pooled_parts, axis=1)
    a = _dot(yd_ref[...], wdo_ref[...])
    b = _dot(y_pool, wpo_ref[...])
    merged = gt_ref[:, :D_MODEL] * a + gt_ref[:, D_MODEL:] * b
    mix = _dot(merged.astype(BF16), wo_ref[...])
    o_ref[...] = h_ref[...] + _rms(mix, g_ref[...])


def _mix_out(h, y_delta, gates, pool_in, seq_len, w_delta_out, w_pool_mix, pool_scale, w_pool_out,
             w_o, post_g):
    t = h.shape[0]
    tm = TOKEN_TILE
    tps = seq_len // tm
    halo_per_tile = tm // POOL_HALO
    n_halo_blocks = t // POOL_HALO

    def row(n):
        return pl.BlockSpec((tm, n), lambda i: (i, 0))

    prev = pl.BlockSpec((POOL_HALO, D_POOL), lambda i: (jnp.maximum(i * halo_per_tile - 1, 0), 0))
    nxt = pl.BlockSpec((POOL_HALO, D_POOL),
                       lambda i: (jnp.minimum((i + 1) * halo_per_tile, n_halo_blocks - 1), 0))
    return pl.pallas_call(
        functools.partial(_mix_out_body, tm=tm, tiles_per_seq=tps, seq_len=seq_len),
        grid=(t // tm,),
        in_specs=[row(D_MODEL), row(D_MODEL), row(2 * D_MODEL), prev, row(D_POOL), nxt,
                  _resident((D_MODEL, D_MODEL)),
                  _resident((len(POOL_WINDOWS), POOL_GROUP_DIM, POOL_GROUP_DIM)),
                  _resident((1, D_POOL)), _resident((D_POOL, D_MODEL)),
                  _resident((D_MODEL, D_MODEL)), _resident((1, D_MODEL))],
        out_specs=row(D_MODEL),
        out_shape=jax.ShapeDtypeStruct((t, D_MODEL), F32),
        scratch_shapes=[pltpu.VMEM((tm + 2 * POOL_HALO, D_POOL), F32)],
        compiler_params=_params(1),
        name="mix_out",
    )(h, y_delta, gates, pool_in, pool_in, pool_in, w_delta_out, w_pool_mix, pool_scale, w_pool_out,
      w_o, post_g)


def _layer(x, w):
    batch, seq_len, _ = x.shape
    assert seq_len % TOKEN_TILE == 0 and seq_len % (DELTA_CHUNKS_PER_STEP * DELTA_CHUNK) == 0
    h0 = x.reshape(batch * seq_len, D_MODEL)
    h1 = _ffn(h0, w["ffn1_pre_g"], w["ffn1_w_in"], w["ffn1_w_out"], w["ffn1_post_g"], w["final_g"],
              apply_final=False)
    q, k, kt, v = _qkv(h1, seq_len, w["mix_pre_g"], w["w_qkv"], w["conv_w"])
    sz, gb, gct, pool_in, gates = _misc(h1, w["mix_pre_g"], w["w_z"], w["w_gb"], w["w_pool"],
                                        w["w_gates"], w["alog_row"], w["dtb_row"])
    o_fwd = _delta(q, k, kt, v, gb, gct, batch, seq_len)
    y_delta = _delta(q, k, kt, v, gb, gct, batch, seq_len, merge=(o_fwd, sz, w["delta_norm_g"]))
    h2 = _mix_out(h1, y_delta, gates, pool_in, seq_len, w["w_delta_out"], w["w_pool_mix"],
                  w["pool_scale"], w["w_pool_out"], w["w_o"], w["mix_post_g"])
    h3 = _ffn(h2, w["ffn2_pre_g"], w["ffn2_w_in"], w["ffn2_w_out"], w["ffn2_post_g"], w["final_g"],
              apply_final=True)
    return h3.reshape(batch, seq_len, D_MODEL)


def _prepare_weights(ffn1_pre_g, ffn1_w_in, ffn1_w_out, ffn1_post_g, mix_pre_g, w_in, conv_w, a_log,
                     dt_bias, delta_norm_g, w_delta_out, w_pool_mix, pool_scale, w_pool_out, w_o,
                     mix_post_g, ffn2_pre_g, ffn2_w_in, ffn2_w_out, ffn2_post_g, final_g):
    def gain(g):
        return g[0].reshape(1, -1).astype(F32)

    def mxu(m):
        return m.astype(BF16)

    n_dh = N_DIRS * N_HEADS
    o_z = 3 * D_MODEL
    o_dec = o_z + D_MODEL
    o_pool = o_dec + 2 * n_dh
    o_gates = o_pool + D_POOL
    wi = w_in[0]
    lane_pad = LANES - 2 * n_dh
    w_gb = jnp.pad(wi[:, o_dec:o_pool], ((0, 0), (0, lane_pad)))
    alog_row = jnp.pad(a_log[0].reshape(1, n_dh), ((0, 0), (0, LANES - n_dh))).astype(F32)
    dtb_row = jnp.pad(dt_bias[0].reshape(1, n_dh), ((0, 0), (0, LANES - n_dh))).astype(F32)
    return dict(
        ffn1_pre_g=gain(ffn1_pre_g), ffn1_w_in=mxu(ffn1_w_in[0]), ffn1_w_out=mxu(ffn1_w_out[0]),
        ffn1_post_g=gain(ffn1_post_g), mix_pre_g=gain(mix_pre_g),
        w_qkv=mxu(wi[:, :o_z]), w_z=mxu(wi[:, o_z:o_dec]), w_gb=mxu(w_gb),
        w_pool=mxu(wi[:, o_pool:o_gates]), w_gates=mxu(wi[:, o_gates:]),
        conv_w=conv_w[0].astype(F32), alog_row=alog_row, dtb_row=dtb_row,
        delta_norm_g=gain(delta_norm_g), w_delta_out=mxu(w_delta_out[0]),
        w_pool_mix=mxu(w_pool_mix[0]), pool_scale=pool_scale[0].reshape(1, D_POOL).astype(F32),
        w_pool_out=mxu(w_pool_out[0]), w_o=mxu(w_o[0]), mix_post_g=gain(mix_post_g),
        ffn2_pre_g=gain(ffn2_pre_g), ffn2_w_in=mxu(ffn2_w_in[0]), ffn2_w_out=mxu(ffn2_w_out[0]),
        ffn2_post_g=gain(ffn2_post_g), final_g=gain(final_g))


def kernel(x_prompt, x_sample, ffn1_pre_g, ffn1_w_in, ffn1_w_out, ffn1_post_g, mix_pre_g, w_in, conv_w,
           a_log, dt_bias, delta_norm_g, w_delta_out, w_pool_mix, pool_scale, w_pool_out, w_o,
           mix_post_g, ffn2_pre_g, ffn2_w_in, ffn2_w_out, ffn2_post_g, final_g):
    assert ffn1_w_in.shape[0] == 1, "single-layer trunk"
    w = _prepare_weights(ffn1_pre_g, ffn1_w_in, ffn1_w_out, ffn1_post_g, mix_pre_g, w_in, conv_w, a_log,
                         dt_bias, delta_norm_g, w_delta_out, w_pool_mix, pool_scale, w_pool_out, w_o,
                         mix_post_g, ffn2_pre_g, ffn2_w_in, ffn2_w_out, ffn2_post_g, final_g)
    return (_layer(x_prompt, w), _layer(x_sample, w))
```

```python
import functools

import jax
import jax.numpy as jnp
from jax import lax
from jax.experimental import pallas as pl
from jax.experimental.pallas import tpu as pltpu

F32 = jnp.float32
BF16 = jnp.bfloat16

D_MODEL = 1024
HEAD_DIM = 128
N_HEADS = D_MODEL // HEAD_DIM
N_DIRS = 2
CONV_WIDTH = 5
POOL_WINDOWS = (2, 4, 8, 16)
POOL_GROUP_DIM = 128
D_POOL = len(POOL_WINDOWS) * POOL_GROUP_DIM
D_FF = 2816
RMS_EPS = 1e-6
L2_EPS = 1e-6

LANES = 128
SUBLANES = 8
CONV_ROW_STRIDE = 4
POOL_ROW_STRIDE = 4
FF_CHUNK = 256
TOKEN_TILE = 512
DELTA_CHUNK = 128
DELTA_CHUNKS_PER_STEP = 4
QKV_HALO = 16
POOL_HALO = 8
VMEM_LIMIT_BYTES = 56 * 1024 * 1024


def _rms(x, g):
    ms = jnp.mean(x * x, axis=-1, keepdims=True)
    return x * lax.rsqrt(ms + RMS_EPS) * g


def _sigmoid(x):
    return 0.5 * jnp.tanh(0.5 * x) + 0.5


def _silu_of_half(half):
    return half + half * jnp.tanh(half)


def _silu(x):
    return _silu_of_half(0.5 * x)


def _dot(a, b):
    return jnp.dot(a, b, preferred_element_type=F32)


def _resident(shape):
    return pl.BlockSpec(shape, lambda *_: (0,) * len(shape), pipeline_mode=pl.Buffered(1))


def _params(n_grid_dims):
    return pltpu.CompilerParams(
        dimension_semantics=("arbitrary",) * n_grid_dims,
        vmem_limit_bytes=VMEM_LIMIT_BYTES)


def _ffn_body(x_ref, pre_g_ref, w_in_ref, w_out_ref, post_g_ref, final_g_ref, o_ref, *, apply_final):
    x = x_ref[...]
    xn = _rms(x, pre_g_ref[...]).astype(BF16)
    acc = jnp.zeros(x.shape, F32)
    for c in range(D_FF // FF_CHUNK):
        lo = c * FF_CHUNK
        gate = _dot(xn, w_in_ref[:, lo:lo + FF_CHUNK])
        up = _dot(xn, w_in_ref[:, D_FF + lo:D_FF + lo + FF_CHUNK])
        act = (_silu(gate) * up).astype(BF16)
        acc = acc + _dot(act, w_out_ref[lo:lo + FF_CHUNK, :])
    h = x + 0.5 * _rms(acc, post_g_ref[...])
    if apply_final:
        h = _rms(h, final_g_ref[...])
    o_ref[...] = h


def _ffn(x, pre_g, w_in, w_out, post_g, final_g, *, apply_final):
    t = x.shape[0]
    tm = TOKEN_TILE
    row = pl.BlockSpec((tm, D_MODEL), lambda i: (i, 0))
    return pl.pallas_call(
        functools.partial(_ffn_body, apply_final=apply_final),
        grid=(t // tm,),
        in_specs=[row, _resident((1, D_MODEL)), _resident((D_MODEL, 2 * D_FF)),
                  _resident((D_FF, D_MODEL)), _resident((1, D_MODEL)), _resident((1, D_MODEL))],
        out_specs=row,
        out_shape=jax.ShapeDtypeStruct((t, D_MODEL), F32),
        compiler_params=_params(1),
        name="ffn_final" if apply_final else "ffn",
    )(x, pre_g, w_in, w_out, post_g, final_g)


def _split3(x):
    a = x.astype(BF16)
    r = x - a.astype(F32)
    b = r.astype(BF16)
    c = (r - b.astype(F32)).astype(BF16)
    return a, b, c


def _in_proj_body(xp_ref, x_ref, xn_ref, g_ref, wqkv_ref, cw_ref, wz_ref, wgb_ref, wp_ref, wg_ref,
                  alog_ref, dtb_ref,
                  q_ref, k_ref, kt_ref, v_ref, sz_ref, gb_ref, gct_ref, p_ref, gt_ref,
                  xe_scr, raw_scr, stage_scr, *, tm, tiles_per_seq):
    t = lax.rem(pl.program_id(0), tiles_per_seq)
    g = g_ref[...]
    keep_prev = (t > 0).astype(F32)
    keep_next = (t < tiles_per_seq - 1).astype(F32)
    xe_scr[0:QKV_HALO, :] = (_rms(xp_ref[...], g) * keep_prev).astype(BF16)
    xe_scr[QKV_HALO:QKV_HALO + tm, :] = _rms(x_ref[...], g).astype(BF16)
    xe_scr[QKV_HALO + tm:, :] = (_rms(xn_ref[...], g) * keep_next).astype(BF16)
    xn = xe_scr[QKV_HALO:QKV_HALO + tm, :]

    def conv_section(s, out_ref):
        raw = _dot(xe_scr[...], wqkv_ref[:, s * D_MODEL:(s + 1) * D_MODEL])
        for h in range(N_HEADS):
            raw_scr[h] = raw[:, h * HEAD_DIM:(h + 1) * HEAD_DIM]
        scale = HEAD_DIM ** -0.5 if s == 0 else 1.0
        pad = CONV_WIDTH // 2
        group = SUBLANES * CONV_ROW_STRIDE
        for h in range(N_HEADS):
            lanes = slice(s * D_MODEL + h * HEAD_DIM, s * D_MODEL + (h + 1) * HEAD_DIM)
            half_taps = [0.5 * cw_ref[i:i + 1, lanes] for i in range(CONV_WIDTH)]
            for first in range(0, tm, group):
                for phase in range(CONV_ROW_STRIDE):
                    half = None
                    for i in range(CONV_WIDTH):
                        start = QKV_HALO - pad + i + first + phase
                        tap = raw_scr[h, pl.ds(start, SUBLANES, stride=CONV_ROW_STRIDE), :] * half_taps[i]
                        half = tap if half is None else half + tap
                    y = _silu_of_half(half)
                    if s < 2:
                        ss = jnp.sum(y * y, axis=-1, keepdims=True)
                        y = y * (lax.rsqrt(ss + L2_EPS) * scale)
                    stage_scr[h, pl.ds(first + phase, SUBLANES, stride=CONV_ROW_STRIDE), :] = y
        for h in range(N_HEADS):
            hs = slice(h * HEAD_DIM, (h + 1) * HEAD_DIM)
            out_ref[:, hs] = stage_scr[h]
            if s == 1:
                kt_ref[hs, :] = stage_scr[h].T

    def gates_columns(lo, hi):
        gt_ref[:, lo:hi] = _sigmoid(_dot(xn, wg_ref[:, lo:hi])).astype(gt_ref.dtype)

    def decay_and_beta():
        raw = _dot(xn, wgb_ref[...])
        sp_in = raw + dtb_ref[...]
        softplus = jnp.maximum(sp_in, 0.0) + jnp.log(1.0 + jnp.exp(-jnp.abs(sp_in)))
        log_decay = -jnp.exp(alog_ref[...]) * softplus
        beta = _sigmoid(raw)
        c = DELTA_CHUNK
        row = lax.broadcasted_iota(jnp.int32, (c, c), 0)
        col = lax.broadcasted_iota(jnp.int32, (c, c), 1)
        ones_lower = (row >= col).astype(BF16)
        ones_upper = (row <= col).astype(BF16)
        lane = lax.broadcasted_iota(jnp.int32, (c, LANES), 1)
        for j in range(tm // c):
            rows = slice(j * c, (j + 1) * c)
            g_fwd = jnp.where(lane < N_HEADS, log_decay[rows], 0.0)
            g_bwd = jnp.where((lane >= N_HEADS) & (lane < N_DIRS * N_HEADS), log_decay[rows], 0.0)
            gc = (sum(_dot(ones_lower, part) for part in _split3(g_fwd))
                  + sum(_dot(ones_upper, part) for part in _split3(g_bwd)))
            gb_ref[rows, :] = jnp.where(lane < N_DIRS * N_HEADS, gc, beta[rows])
            gct_ref[:, rows] = gc.T

    conv_section(0, q_ref)
    sz_ref[...] = _silu_of_half(0.5 * _dot(xn, wz_ref[...]))
    conv_section(1, k_ref)
    gates_columns(0, D_MODEL)
    conv_section(2, v_ref)
    gates_columns(D_MODEL, 2 * D_MODEL)
    p_ref[...] = _dot(xn, wp_ref[...])
    decay_and_beta()


def _in_proj(h, seq_len, g, w_qkv, conv_w, w_z, w_gb, w_pool, w_gates, alog_row, dtb_row):
    t = h.shape[0]
    tm = TOKEN_TILE
    tps = seq_len // tm
    halo_per_tile = tm // QKV_HALO
    n_halo_blocks = t // QKV_HALO

    def row(n):
        return pl.BlockSpec((tm, n), lambda i: (i, 0))

    def col(n):
        return pl.BlockSpec((n, tm), lambda i: (0, i))

    def out(n, dtype=F32):
        return jax.ShapeDtypeStruct((t, n), dtype)

    def out_t(n):
        return jax.ShapeDtypeStruct((n, t), F32)

    prev = pl.BlockSpec((QKV_HALO, D_MODEL), lambda i: (jnp.maximum(i * halo_per_tile - 1, 0), 0))
    nxt = pl.BlockSpec((QKV_HALO, D_MODEL),
                       lambda i: (jnp.minimum((i + 1) * halo_per_tile, n_halo_blocks - 1), 0))
    return pl.pallas_call(
        functools.partial(_in_proj_body, tm=tm, tiles_per_seq=tps),
        grid=(t // tm,),
        in_specs=[prev, row(D_MODEL), nxt, _resident((1, D_MODEL)), _resident((D_MODEL, 3 * D_MODEL)),
                  _resident((CONV_WIDTH, 3 * D_MODEL)), _resident((D_MODEL, D_MODEL)),
                  _resident((D_MODEL, LANES)), _resident((D_MODEL, D_POOL)),
                  _resident((D_MODEL, 2 * D_MODEL)), _resident((1, LANES)), _resident((1, LANES))],
        out_specs=[row(D_MODEL), row(D_MODEL), col(D_MODEL), row(D_MODEL), row(D_MODEL), row(LANES),
                   col(LANES), row(D_POOL), row(2 * D_MODEL)],
        out_shape=[out(D_MODEL), out(D_MODEL), out_t(D_MODEL), out(D_MODEL), out(D_MODEL), out(LANES),
                   out_t(LANES), out(D_POOL), out(2 * D_MODEL, BF16)],
        scratch_shapes=[pltpu.VMEM((tm + 2 * QKV_HALO, D_MODEL), BF16),
                        pltpu.VMEM((N_HEADS, tm + 2 * QKV_HALO, HEAD_DIM), F32),
                        pltpu.VMEM((N_HEADS, tm, HEAD_DIM), F32)],
        compiler_params=_params(1),
        name="in_proj",
    )(h, h, h, g, w_qkv, conv_w, w_z, w_gb, w_pool, w_gates, alog_row, dtb_row)


def _delta_body(q_ref, k_ref, kt_ref, v_ref, gb_ref, gct_ref, *rest, reverse, n_chunks):
    if reverse:
        ofwd_ref, sz_ref, ng_ref, out_ref, s_scr = rest
    else:
        out_ref, s_scr = rest
    c = DELTA_CHUNK

    @pl.when(pl.program_id(1) == 0)
    def _():
        s_scr[...] = jnp.zeros(s_scr.shape, F32)

    row = lax.broadcasted_iota(jnp.int32, (c, c), 0)
    col = lax.broadcasted_iota(jnp.int32, (c, c), 1)
    incl = (row <= col) if reverse else (row >= col)
    strict = (row < col) if reverse else (row > col)
    last = 0 if reverse else c - 1
    direction = 1 if reverse else 0
    n_steps = c.bit_length() - 2
    heads = range(N_HEADS)

    def chunk_local(g):
        rows = slice(g * c, (g + 1) * c)
        loc = []
        for h in heads:
            hs = slice(h * HEAD_DIM, (h + 1) * HEAD_DIM)
            cg = direction * N_HEADS + h
            cb = N_DIRS * N_HEADS + cg
            gc_c = gb_ref[rows, cg:cg + 1]
            gc_r = gct_ref[cg:cg + 1, rows]
            beta_c = gb_ref[rows, cb:cb + 1]
            decay = jnp.exp(jnp.where(incl, gc_c - gc_r, -jnp.inf))
            e_c = jnp.exp(gc_c)
            q = q_ref[rows, hs]
            k = k_ref[rows, hs]
            v = v_ref[rows, hs]
            kt = kt_ref[hs, rows]
            kb = k * beta_c
            pr = _dot(jnp.concatenate([kb, q], axis=0).astype(BF16), kt.astype(BF16))
            loc.append(dict(
                lmat=jnp.where(strict, pr[:c] * decay, 0.0), qk=(pr[c:] * decay).astype(BF16),
                rhs=jnp.concatenate([v * beta_c, kb * e_c], axis=1),
                qe=(q * e_c).astype(BF16),
                k_decay_t=(kt * decay[last:last + 1, :]).astype(BF16),
                state_decay=jnp.exp(gc_r[:, last:last + 1])))

        for d in loc:
            lb = d["lmat"].astype(BF16)
            d["m"] = _dot(lb, lb)
            d["r"] = -d.pop("lmat")
        for step in range(1, n_steps + 1):
            for d in loc:
                r, m = d["r"], d["m"]
                mb = m.astype(BF16)
                if step < n_steps:
                    pr2 = _dot(jnp.concatenate([r, m], axis=0).astype(BF16), mb)
                    d["r"] = r + m + pr2[:c]
                    d["m"] = pr2[c:]
                else:
                    d["r"] = r + m + _dot(r.astype(BF16), mb)
        for d in loc:
            rhs = d.pop("rhs")
            sol = rhs + _dot(d.pop("r").astype(BF16), rhs.astype(BF16))
            d["u"] = sol[:, :HEAD_DIM]
            d["w"] = sol[:, HEAD_DIM:].astype(BF16)
        return loc

    states = [s_scr[h] for h in heads]
    for g in (reversed(range(n_chunks)) if reverse else range(n_chunks)):
        rows = slice(g * c, (g + 1) * c)
        loc = chunk_local(g)
        pr3 = [_dot(jnp.concatenate([loc[h]["w"], loc[h]["qe"]], axis=0), states[h].astype(BF16))
               for h in heads]
        v_new = [(loc[h]["u"] - pr3[h][:c]).astype(BF16) for h in heads]
        pr4 = [_dot(jnp.concatenate([loc[h]["qk"], loc[h]["k_decay_t"]], axis=0), v_new[h])
               for h in heads]
        states = [states[h] * loc[h]["state_decay"] + pr4[h][c:] for h in heads]
        for h in heads:
            hs = slice(h * HEAD_DIM, (h + 1) * HEAD_DIM)
            o = pr3[h][c:] + pr4[h][:c]
            if reverse:
                tot = o + ofwd_ref[rows, hs]
                ms = jnp.mean(tot * tot, axis=-1, keepdims=True)
                y = tot * lax.rsqrt(ms + RMS_EPS) * ng_ref[...] * sz_ref[rows, hs]
                out_ref[rows, hs] = y.astype(out_ref.dtype)
            else:
                out_ref[rows, hs] = o
    for h in heads:
        s_scr[h] = states[h]


def _delta(q, k, kt, v, gb, gct, batch, seq_len, merge=None):
    reverse = merge is not None
    t = q.shape[0]
    c = DELTA_CHUNKS_PER_STEP * DELTA_CHUNK
    n = seq_len // c

    def blk(b, j):
        return (b * n + (n - 1 - j if reverse else j), 0)

    def row(width):
        return pl.BlockSpec((c, width), blk)

    def col(height):
        return pl.BlockSpec((height, c), lambda b, j: blk(b, j)[::-1])

    in_specs = [row(D_MODEL), row(D_MODEL), col(D_MODEL), row(D_MODEL), row(LANES), col(LANES)]
    operands = [q, k, kt, v, gb, gct]
    if reverse:
        in_specs += [row(D_MODEL), row(D_MODEL), _resident((1, HEAD_DIM))]
        operands += list(merge)
    return pl.pallas_call(
        functools.partial(_delta_body, reverse=reverse, n_chunks=DELTA_CHUNKS_PER_STEP),
        grid=(batch, n),
        in_specs=in_specs,
        out_specs=row(D_MODEL),
        out_shape=jax.ShapeDtypeStruct((t, D_MODEL), BF16 if reverse else F32),
        scratch_shapes=[pltpu.VMEM((N_HEADS, HEAD_DIM, HEAD_DIM), F32)],
        compiler_params=_params(2),
        name="delta_bwd" if reverse else "delta_fwd",
    )(*operands)


def _mix_out_body(h_ref, yd_ref, gt_ref, pp_ref, p_ref, pn_ref, wdo_ref, wpm_ref, ps_ref, wpo_ref,
                  wo_ref, g_ref, o_ref, pad_scr, sum_scr, *, tm, tiles_per_seq, seq_len):
    t = lax.rem(pl.program_id(0), tiles_per_seq)
    keep_prev = (t > 0).astype(F32)
    keep_next = (t < tiles_per_seq - 1).astype(F32)
    for gi in range(len(POOL_WINDOWS)):
        cols = slice(gi * POOL_GROUP_DIM, (gi + 1) * POOL_GROUP_DIM)
        pad_scr[gi, 0:POOL_HALO, :] = pp_ref[:, cols] * keep_prev
        pad_scr[gi, POOL_HALO:POOL_HALO + tm, :] = p_ref[:, cols]
        pad_scr[gi, POOL_HALO + tm:, :] = pn_ref[:, cols] * keep_next
    pos = t * tm + lax.broadcasted_iota(jnp.int32, (tm, 1), 0)
    group = SUBLANES * POOL_ROW_STRIDE
    pooled_parts = []
    for gi, win in enumerate(POOL_WINDOWS):
        cols = slice(gi * POOL_GROUP_DIM, (gi + 1) * POOL_GROUP_DIM)
        back = win // 2
        fwd = win - back
        for first in range(0, tm, group):
            for phase in range(POOL_ROW_STRIDE):
                total = None
                for off in range(-back, fwd):
                    start = POOL_HALO + off + first + phase
                    part = pad_scr[gi, pl.ds(start, SUBLANES, stride=POOL_ROW_STRIDE), :]
                    total = part if total is None else total + part
                sum_scr[gi, pl.ds(first + phase, SUBLANES, stride=POOL_ROW_STRIDE), :] = total
        cnt = (jnp.minimum(pos + fwd, seq_len) - jnp.maximum(pos - back, 0)).astype(F32)
        pooled = sum_scr[gi] / cnt - p_ref[:, cols]
        yp = _dot(pooled.astype(BF16), wpm_ref[gi]) * ps_ref[:, cols]
        pooled_parts.append(yp.astype(BF16))
    y_pool = jnp.concatenate(pooled_parts, axis=1)
    a = _dot(yd_ref[...], wdo_ref[...])
    b = _dot(y_pool, wpo_ref[...])
    merged = gt_ref[:, :D_MODEL] * a + gt_ref[:, D_MODEL:] * b
    mix = _dot(merged.astype(BF16), wo_ref[...])
    o_ref[...] = h_ref[...] + _rms(mix, g_ref[...])


def _mix_out(h, y_delta, gates, pool_in, seq_len, w_delta_out, w_pool_mix, pool_scale, w_pool_out,
             w_o, post_g):
    t = h.shape[0]
    tm = TOKEN_TILE
    tps = seq_len // tm
    halo_per_tile = tm // POOL_HALO
    n_halo_blocks = t // POOL_HALO

    def row(n):
        return pl.BlockSpec((tm, n), lambda i: (i, 0))

    prev = pl.BlockSpec((POOL_HALO, D_POOL), lambda i: (jnp.maximum(i * halo_per_tile - 1, 0), 0))
    nxt = pl.BlockSpec((POOL_HALO, D_POOL),
                       lambda i: (jnp.minimum((i + 1) * halo_per_tile, n_halo_blocks - 1), 0))
    return pl.pallas_call(
        functools.partial(_mix_out_body, tm=tm, tiles_per_seq=tps, seq_len=seq_len),
        grid=(t // tm,),
        in_specs=[row(D_MODEL), row(D_MODEL), row(2 * D_MODEL), prev, row(D_POOL), nxt,
                  _resident((D_MODEL, D_MODEL)),
                  _resident((len(POOL_WINDOWS), POOL_GROUP_DIM, POOL_GROUP_DIM)),
                  _resident((1, D_POOL)), _resident((D_POOL, D_MODEL)),
                  _resident((D_MODEL, D_MODEL)), _resident((1, D_MODEL))],
        out_specs=row(D_MODEL),
        out_shape=jax.ShapeDtypeStruct((t, D_MODEL), F32),
        scratch_shapes=[pltpu.VMEM((len(POOL_WINDOWS), tm + 2 * POOL_HALO, POOL_GROUP_DIM), F32),
                        pltpu.VMEM((len(POOL_WINDOWS), tm, POOL_GROUP_DIM), F32)],
        compiler_params=_params(1),
        name="mix_out",
    )(h, y_delta, gates, pool_in, pool_in, pool_in, w_delta_out, w_pool_mix, pool_scale, w_pool_out,
      w_o, post_g)


def _layer(x, w):
    batch, seq_len, _ = x.shape
    assert seq_len % TOKEN_TILE == 0 and seq_len % (DELTA_CHUNKS_PER_STEP * DELTA_CHUNK) == 0
    h0 = x.reshape(batch * seq_len, D_MODEL)
    h1 = _ffn(h0, w["ffn1_pre_g"], w["ffn1_w_in"], w["ffn1_w_out"], w["ffn1_post_g"], w["final_g"],
              apply_final=False)
    q, k, kt, v, sz, gb, gct, pool_in, gates = _in_proj(
        h1, seq_len, w["mix_pre_g"], w["w_qkv"], w["conv_w"], w["w_z"], w["w_gb"], w["w_pool"],
        w["w_gates"], w["alog_row"], w["dtb_row"])
    o_fwd = _delta(q, k, kt, v, gb, gct, batch, seq_len)
    y_delta = _delta(q, k, kt, v, gb, gct, batch, seq_len, merge=(o_fwd, sz, w["delta_norm_g"]))
    h2 = _mix_out(h1, y_delta, gates, pool_in, seq_len, w["w_delta_out"], w["w_pool_mix"],
                  w["pool_scale"], w["w_pool_out"], w["w_o"], w["mix_post_g"])
    h3 = _ffn(h2, w["ffn2_pre_g"], w["ffn2_w_in"], w["ffn2_w_out"], w["ffn2_post_g"], w["final_g"],
              apply_final=True)
    return h3.reshape(batch, seq_len, D_MODEL)


def _prepare_weights(ffn1_pre_g, ffn1_w_in, ffn1_w_out, ffn1_post_g, mix_pre_g, w_in, conv_w, a_log,
                     dt_bias, delta_norm_g, w_delta_out, w_pool_mix, pool_scale, w_pool_out, w_o,
                     mix_post_g, ffn2_pre_g, ffn2_w_in, ffn2_w_out, ffn2_post_g, final_g):
    def gain(g):
        return g[0].reshape(1, -1).astype(F32)

    def mxu(m):
        return m.astype(BF16)

    n_dh = N_DIRS * N_HEADS
    o_z = 3 * D_MODEL
    o_dec = o_z + D_MODEL
    o_pool = o_dec + 2 * n_dh
    o_gates = o_pool + D_POOL
    wi = w_in[0]
    lane_pad = LANES - 2 * n_dh
    w_gb = jnp.pad(wi[:, o_dec:o_pool], ((0, 0), (0, lane_pad)))
    alog_row = jnp.pad(a_log[0].reshape(1, n_dh), ((0, 0), (0, LANES - n_dh))).astype(F32)
    dtb_row = jnp.pad(dt_bias[0].reshape(1, n_dh), ((0, 0), (0, LANES - n_dh))).astype(F32)
    return dict(
        ffn1_pre_g=gain(ffn1_pre_g), ffn1_w_in=mxu(ffn1_w_in[0]), ffn1_w_out=mxu(ffn1_w_out[0]),
        ffn1_post_g=gain(ffn1_post_g), mix_pre_g=gain(mix_pre_g),
        w_qkv=mxu(wi[:, :o_z]), w_z=mxu(wi[:, o_z:o_dec]), w_gb=mxu(w_gb),
        w_pool=mxu(wi[:, o_pool:o_gates]), w_gates=mxu(wi[:, o_gates:]),
        conv_w=conv_w[0].astype(F32), alog_row=alog_row, dtb_row=dtb_row,
        delta_norm_g=gain(delta_norm_g), w_delta_out=mxu(w_delta_out[0]),
        w_pool_mix=mxu(w_pool_mix[0]), pool_scale=pool_scale[0].reshape(1, D_POOL).astype(F32),
        w_pool_out=mxu(w_pool_out[0]), w_o=mxu(w_o[0]), mix_post_g=gain(mix_post_g),
        ffn2_pre_g=gain(ffn2_pre_g), ffn2_w_in=mxu(ffn2_w_in[0]), ffn2_w_out=mxu(ffn2_w_out[0]),
        ffn2_post_g=gain(ffn2_post_g), final_g=gain(final_g))


def kernel(x_prompt, x_sample, ffn1_pre_g, ffn1_w_in, ffn1_w_out, ffn1_post_g, mix_pre_g, w_in, conv_w,
           a_log, dt_bias, delta_norm_g, w_delta_out, w_pool_mix, pool_scale, w_pool_out, w_o,
           mix_post_g, ffn2_pre_g, ffn2_w_in, ffn2_w_out, ffn2_post_g, final_g):
    assert ffn1_w_in.shape[0] == 1, "single-layer trunk"
    w = _prepare_weights(ffn1_pre_g, ffn1_w_in, ffn1_w_out, ffn1_post_g, mix_pre_g, w_in, conv_w, a_log,
                         dt_bias, delta_norm_g, w_delta_out, w_pool_mix, pool_scale, w_pool_out, w_o,
                         mix_post_g, ffn2_pre_g, ffn2_w_in, ffn2_w_out, ffn2_post_g, final_g)
    return (_layer(x_prompt, w), _layer(x_sample, w))
```

```python
import functools

import jax
import jax.numpy as jnp
from jax import lax
from jax.experimental import pallas as pl
from jax.experimental.pallas import tpu as pltpu

F32 = jnp.float32
BF16 = jnp.bfloat16

D_MODEL = 1024
HEAD_DIM = 128
N_HEADS = D_MODEL // HEAD_DIM
N_DIRS = 2
CONV_WIDTH = 5
POOL_WINDOWS = (2, 4, 8, 16)
POOL_GROUP_DIM = 128
D_POOL = len(POOL_WINDOWS) * POOL_GROUP_DIM
D_FF = 2816
RMS_EPS = 1e-6
L2_EPS = 1e-6

LANES = 128
SUBLANES = 8
CONV_ROW_STRIDE = 4
POOL_ROW_STRIDE = 4
FF_CHUNK = 256
TOKEN_TILE = 512
DELTA_CHUNK = 128
DELTA_CHUNKS_PER_STEP = 4
QKV_HALO = 16
POOL_HALO = 8
VMEM_LIMIT_BYTES = 56 * 1024 * 1024


def _rms(x, g):
    ms = jnp.mean(x * x, axis=-1, keepdims=True)
    return x * lax.rsqrt(ms + RMS_EPS) * g


def _sigmoid(x):
    return 0.5 * jnp.tanh(0.5 * x) + 0.5


def _silu_of_half(half):
    return half + half * jnp.tanh(half)


def _silu(x):
    return _silu_of_half(0.5 * x)


def _dot(a, b):
    return jnp.dot(a, b, preferred_element_type=F32)


def _resident(shape):
    return pl.BlockSpec(shape, lambda *_: (0,) * len(shape), pipeline_mode=pl.Buffered(1))


def _params(n_grid_dims):
    return pltpu.CompilerParams(
        dimension_semantics=("arbitrary",) * n_grid_dims,
        vmem_limit_bytes=VMEM_LIMIT_BYTES)


def _ffn_body(x_ref, pre_g_ref, w_in_ref, w_out_ref, post_g_ref, final_g_ref, o_ref, *, apply_final):
    x = x_ref[...]
    xn = _rms(x, pre_g_ref[...]).astype(BF16)
    acc = jnp.zeros(x.shape, F32)
    for c in range(D_FF // FF_CHUNK):
        lo = c * FF_CHUNK
        gate = _dot(xn, w_in_ref[:, lo:lo + FF_CHUNK])
        up = _dot(xn, w_in_ref[:, D_FF + lo:D_FF + lo + FF_CHUNK])
        act = (_silu(gate) * up).astype(BF16)
        acc = acc + _dot(act, w_out_ref[lo:lo + FF_CHUNK, :])
    h = x + 0.5 * _rms(acc, post_g_ref[...])
    if apply_final:
        h = _rms(h, final_g_ref[...])
    o_ref[...] = h


def _ffn(x, pre_g, w_in, w_out, post_g, final_g, *, apply_final):
    t = x.shape[0]
    tm = TOKEN_TILE
    row = pl.BlockSpec((tm, D_MODEL), lambda i: (i, 0))
    return pl.pallas_call(
        functools.partial(_ffn_body, apply_final=apply_final),
        grid=(t // tm,),
        in_specs=[row, _resident((1, D_MODEL)), _resident((D_MODEL, 2 * D_FF)),
                  _resident((D_FF, D_MODEL)), _resident((1, D_MODEL)), _resident((1, D_MODEL))],
        out_specs=row,
        out_shape=jax.ShapeDtypeStruct((t, D_MODEL), F32),
        compiler_params=_params(1),
        name="ffn_final" if apply_final else "ffn",
    )(x, pre_g, w_in, w_out, post_g, final_g)


def _split3(x):
    a = x.astype(BF16)
    r = x - a.astype(F32)
    b = r.astype(BF16)
    c = (r - b.astype(F32)).astype(BF16)
    return a, b, c


def _in_proj_body(xp_ref, x_ref, xn_ref, g_ref, wqkv_ref, cw_ref, wz_ref, wgb_ref, wp_ref, wg_ref,
                  alog_ref, dtb_ref,
                  q_ref, k_ref, kt_ref, v_ref, sz_ref, gb_ref, gct_ref, p_ref, gt_ref,
                  xe_scr, raw_scr, *, tm, tiles_per_seq):
    t = lax.rem(pl.program_id(0), tiles_per_seq)
    g = g_ref[...]
    keep_prev = (t > 0).astype(F32)
    keep_next = (t < tiles_per_seq - 1).astype(F32)
    xe_scr[0:QKV_HALO, :] = (_rms(xp_ref[...], g) * keep_prev).astype(BF16)
    xe_scr[QKV_HALO:QKV_HALO + tm, :] = _rms(x_ref[...], g).astype(BF16)
    xe_scr[QKV_HALO + tm:, :] = (_rms(xn_ref[...], g) * keep_next).astype(BF16)
    xn = xe_scr[QKV_HALO:QKV_HALO + tm, :]

    def conv_section(s, out_ref):
        raw = _dot(xe_scr[...], wqkv_ref[:, s * D_MODEL:(s + 1) * D_MODEL])
        for h in range(N_HEADS):
            raw_scr[h] = raw[:, h * HEAD_DIM:(h + 1) * HEAD_DIM]
        scale = HEAD_DIM ** -0.5 if s == 0 else 1.0
        pad = CONV_WIDTH // 2
        group = SUBLANES * CONV_ROW_STRIDE
        for h in range(N_HEADS):
            lanes = slice(s * D_MODEL + h * HEAD_DIM, s * D_MODEL + (h + 1) * HEAD_DIM)
            half_taps = [0.5 * cw_ref[i:i + 1, lanes] for i in range(CONV_WIDTH)]
            for first in range(0, tm, group):
                window = [raw_scr[h, pl.ds(QKV_HALO - pad + first + j, SUBLANES, stride=CONV_ROW_STRIDE), :]
                          for j in range(CONV_ROW_STRIDE + CONV_WIDTH - 1)]
                for phase in range(CONV_ROW_STRIDE):
                    half = None
                    for i in range(CONV_WIDTH):
                        tap = window[phase + i] * half_taps[i]
                        half = tap if half is None else half + tap
                    y = _silu_of_half(half)
                    if s < 2:
                        ss = jnp.sum(y * y, axis=-1, keepdims=True)
                        y = y * (lax.rsqrt(ss + L2_EPS) * scale)
                    out_ref[h, pl.ds(first + phase, SUBLANES, stride=CONV_ROW_STRIDE), :] = y
        if s == 1:
            for h in range(N_HEADS):
                kt_ref[h * HEAD_DIM:(h + 1) * HEAD_DIM, :] = out_ref[h].T

    def gates_columns(lo, hi):
        gt_ref[:, lo:hi] = _sigmoid(_dot(xn, wg_ref[:, lo:hi])).astype(gt_ref.dtype)

    def decay_and_beta():
        raw = _dot(xn, wgb_ref[...])
        sp_in = raw + dtb_ref[...]
        softplus = jnp.maximum(sp_in, 0.0) + jnp.log(1.0 + jnp.exp(-jnp.abs(sp_in)))
        log_decay = -jnp.exp(alog_ref[...]) * softplus
        beta = _sigmoid(raw)
        c = DELTA_CHUNK
        row = lax.broadcasted_iota(jnp.int32, (c, c), 0)
        col = lax.broadcasted_iota(jnp.int32, (c, c), 1)
        ones_lower = (row >= col).astype(BF16)
        ones_upper = (row <= col).astype(BF16)
        lane = lax.broadcasted_iota(jnp.int32, (c, LANES), 1)
        for j in range(tm // c):
            rows = slice(j * c, (j + 1) * c)
            g_fwd = jnp.where(lane < N_HEADS, log_decay[rows], 0.0)
            g_bwd = jnp.where((lane >= N_HEADS) & (lane < N_DIRS * N_HEADS), log_decay[rows], 0.0)
            gc = (sum(_dot(ones_lower, part) for part in _split3(g_fwd))
                  + sum(_dot(ones_upper, part) for part in _split3(g_bwd)))
            gb_ref[rows, :] = jnp.where(lane < N_DIRS * N_HEADS, gc, beta[rows])
            gct_ref[:, rows] = gc.T

    conv_section(0, q_ref)
    sz_ref[...] = _silu_of_half(0.5 * _dot(xn, wz_ref[...]))
    conv_section(1, k_ref)
    gates_columns(0, D_MODEL)
    conv_section(2, v_ref)
    gates_columns(D_MODEL, 2 * D_MODEL)
    p_ref[...] = _dot(xn, wp_ref[...])
    decay_and_beta()


def _in_proj(h, seq_len, g, w_qkv, conv_w, w_z, w_gb, w_pool, w_gates, alog_row, dtb_row):
    t = h.shape[0]
    tm = TOKEN_TILE
    tps = seq_len // tm
    halo_per_tile = tm // QKV_HALO
    n_halo_blocks = t // QKV_HALO

    def row(n):
        return pl.BlockSpec((tm, n), lambda i: (i, 0))

    def col(n):
        return pl.BlockSpec((n, tm), lambda i: (0, i))

    def out(n, dtype=F32):
        return jax.ShapeDtypeStruct((t, n), dtype)

    def out_t(n):
        return jax.ShapeDtypeStruct((n, t), F32)

    heads_spec = pl.BlockSpec((N_HEADS, tm, HEAD_DIM), lambda i: (0, i, 0))
    heads_shape = jax.ShapeDtypeStruct((N_HEADS, t, HEAD_DIM), F32)

    prev = pl.BlockSpec((QKV_HALO, D_MODEL), lambda i: (jnp.maximum(i * halo_per_tile - 1, 0), 0))
    nxt = pl.BlockSpec((QKV_HALO, D_MODEL),
                       lambda i: (jnp.minimum((i + 1) * halo_per_tile, n_halo_blocks - 1), 0))
    return pl.pallas_call(
        functools.partial(_in_proj_body, tm=tm, tiles_per_seq=tps),
        grid=(t // tm,),
        in_specs=[prev, row(D_MODEL), nxt, _resident((1, D_MODEL)), _resident((D_MODEL, 3 * D_MODEL)),
                  _resident((CONV_WIDTH, 3 * D_MODEL)), _resident((D_MODEL, D_MODEL)),
                  _resident((D_MODEL, LANES)), _resident((D_MODEL, D_POOL)),
                  _resident((D_MODEL, 2 * D_MODEL)), _resident((1, LANES)), _resident((1, LANES))],
        out_specs=[heads_spec, heads_spec, col(D_MODEL), heads_spec, row(D_MODEL), row(LANES),
                   col(LANES), row(D_POOL), row(2 * D_MODEL)],
        out_shape=[heads_shape, heads_shape, out_t(D_MODEL), heads_shape, out(D_MODEL), out(LANES),
                   out_t(LANES), out(D_POOL), out(2 * D_MODEL, BF16)],
        scratch_shapes=[pltpu.VMEM((tm + 2 * QKV_HALO, D_MODEL), BF16),
                        pltpu.VMEM((N_HEADS, tm + 2 * QKV_HALO, HEAD_DIM), F32)],
        compiler_params=_params(1),
        name="in_proj",
    )(h, h, h, g, w_qkv, conv_w, w_z, w_gb, w_pool, w_gates, alog_row, dtb_row)


def _delta_body(q_ref, k_ref, kt_ref, v_ref, gb_ref, gct_ref, *rest, reverse, n_chunks):
    if reverse:
        ofwd_ref, sz_ref, ng_ref, out_ref, s_scr = rest
    else:
        out_ref, s_scr = rest
    c = DELTA_CHUNK

    @pl.when(pl.program_id(1) == 0)
    def _():
        s_scr[...] = jnp.zeros(s_scr.shape, F32)

    row = lax.broadcasted_iota(jnp.int32, (c, c), 0)
    col = lax.broadcasted_iota(jnp.int32, (c, c), 1)
    incl = (row <= col) if reverse else (row >= col)
    strict = (row < col) if reverse else (row > col)
    last = 0 if reverse else c - 1
    direction = 1 if reverse else 0
    n_steps = c.bit_length() - 2
    heads = range(N_HEADS)

    pairs = [(g, h) for g in range(n_chunks) for h in heads]
    loc = {}
    for g, h in pairs:
        rows = slice(g * c, (g + 1) * c)
        hs = slice(h * HEAD_DIM, (h + 1) * HEAD_DIM)
        cg = direction * N_HEADS + h
        cb = N_DIRS * N_HEADS + cg
        gc_c = gb_ref[rows, cg:cg + 1]
        gc_r = gct_ref[cg:cg + 1, rows]
        beta_c = gb_ref[rows, cb:cb + 1]
        decay = jnp.exp(jnp.where(incl, gc_c - gc_r, -jnp.inf))
        e_c = jnp.exp(gc_c)
        q = q_ref[h, rows, :]
        k = k_ref[h, rows, :]
        v = v_ref[h, rows, :]
        kt = kt_ref[hs, rows]
        kb = k * beta_c
        pr = _dot(jnp.concatenate([kb, q], axis=0).astype(BF16), kt.astype(BF16))
        loc[g, h] = dict(
            lmat=jnp.where(strict, pr[:c] * decay, 0.0), qk=(pr[c:] * decay).astype(BF16),
            rhs=jnp.concatenate([v * beta_c, kb * e_c], axis=1),
            qe=(q * e_c).astype(BF16),
            k_decay_t=(kt * decay[last:last + 1, :]).astype(BF16),
            state_decay=jnp.exp(gc_r[:, last:last + 1]))

    for d in loc.values():
        lb = d["lmat"].astype(BF16)
        d["m"] = _dot(lb, lb)
        d["r"] = -d.pop("lmat")
    for step in range(1, n_steps + 1):
        for d in loc.values():
            r, m = d["r"], d["m"]
            mb = m.astype(BF16)
            if step < n_steps:
                pr2 = _dot(jnp.concatenate([r, m], axis=0).astype(BF16), mb)
                d["r"] = r + m + pr2[:c]
                d["m"] = pr2[c:]
            else:
                d["r"] = r + m + _dot(r.astype(BF16), mb)
    for d in loc.values():
        rhs = d.pop("rhs")
        sol = rhs + _dot(d.pop("r").astype(BF16), rhs.astype(BF16))
        d["u"] = sol[:, :HEAD_DIM]
        d["w"] = sol[:, HEAD_DIM:].astype(BF16)

    states = [s_scr[h] for h in heads]
    for g in (reversed(range(n_chunks)) if reverse else range(n_chunks)):
        rows = slice(g * c, (g + 1) * c)
        pr3 = [_dot(jnp.concatenate([loc[g, h]["w"], loc[g, h]["qe"]], axis=0), states[h].astype(BF16))
               for h in heads]
        v_new = [(loc[g, h]["u"] - pr3[h][:c]).astype(BF16) for h in heads]
        pr4 = [_dot(jnp.concatenate([loc[g, h]["qk"], loc[g, h]["k_decay_t"]], axis=0), v_new[h])
               for h in heads]
        states = [states[h] * loc[g, h]["state_decay"] + pr4[h][c:] for h in heads]
        for h in heads:
            hs = slice(h * HEAD_DIM, (h + 1) * HEAD_DIM)
            o = pr3[h][c:] + pr4[h][:c]
            if reverse:
                tot = o + ofwd_ref[rows, hs]
                ms = jnp.mean(tot * tot, axis=-1, keepdims=True)
                y = tot * lax.rsqrt(ms + RMS_EPS) * ng_ref[...] * sz_ref[rows, hs]
                out_ref[rows, hs] = y.astype(out_ref.dtype)
            else:
                out_ref[rows, hs] = o
    for h in heads:
        s_scr[h] = states[h]


def _delta(q, k, kt, v, gb, gct, batch, seq_len, merge=None):
    reverse = merge is not None
    t = gb.shape[0]
    c = DELTA_CHUNKS_PER_STEP * DELTA_CHUNK
    n = seq_len // c

    def blk(b, j):
        return (b * n + (n - 1 - j if reverse else j), 0)

    def row(width):
        return pl.BlockSpec((c, width), blk)

    def col(height):
        return pl.BlockSpec((height, c), lambda b, j: blk(b, j)[::-1])

    heads = pl.BlockSpec((N_HEADS, c, HEAD_DIM), lambda b, j: (0,) + blk(b, j))
    in_specs = [heads, heads, col(D_MODEL), heads, row(LANES), col(LANES)]
    operands = [q, k, kt, v, gb, gct]
    if reverse:
        in_specs += [row(D_MODEL), row(D_MODEL), _resident((1, HEAD_DIM))]
        operands += list(merge)
    return pl.pallas_call(
        functools.partial(_delta_body, reverse=reverse, n_chunks=DELTA_CHUNKS_PER_STEP),
        grid=(batch, n),
        in_specs=in_specs,
        out_specs=row(D_MODEL),
        out_shape=jax.ShapeDtypeStruct((t, D_MODEL), BF16 if reverse else F32),
        scratch_shapes=[pltpu.VMEM((N_HEADS, HEAD_DIM, HEAD_DIM), F32)],
        compiler_params=_params(2),
        name="delta_bwd" if reverse else "delta_fwd",
    )(*operands)


def _mix_out_body(h_ref, yd_ref, gt_ref, pp_ref, p_ref, pn_ref, wdo_ref, wpm_ref, ps_ref, wpo_ref,
                  wo_ref, g_ref, o_ref, pad_scr, sum_scr, *, tm, tiles_per_seq, seq_len):
    t = lax.rem(pl.program_id(0), tiles_per_seq)
    keep_prev = (t > 0).astype(F32)
    keep_next = (t < tiles_per_seq - 1).astype(F32)
    for gi in range(len(POOL_WINDOWS)):
        cols = slice(gi * POOL_GROUP_DIM, (gi + 1) * POOL_GROUP_DIM)
        pad_scr[gi, 0:POOL_HALO, :] = pp_ref[:, cols] * keep_prev
        pad_scr[gi, POOL_HALO:POOL_HALO + tm, :] = p_ref[:, cols]
        pad_scr[gi, POOL_HALO + tm:, :] = pn_ref[:, cols] * keep_next
    pos = t * tm + lax.broadcasted_iota(jnp.int32, (tm, 1), 0)
    group = SUBLANES * POOL_ROW_STRIDE
    pooled_parts = []
    for gi, win in enumerate(POOL_WINDOWS):
        cols = slice(gi * POOL_GROUP_DIM, (gi + 1) * POOL_GROUP_DIM)
        back = win // 2
        fwd = win - back
        for first in range(0, tm, group):
            window = [pad_scr[gi, pl.ds(POOL_HALO - back + first + j, SUBLANES, stride=POOL_ROW_STRIDE), :]
                      for j in range(POOL_ROW_STRIDE + win - 1)]
            for phase in range(POOL_ROW_STRIDE):
                total = window[phase]
                for j in range(1, win):
                    total = total + window[phase + j]
                sum_scr[gi, pl.ds(first + phase, SUBLANES, stride=POOL_ROW_STRIDE), :] = total
        cnt = (jnp.minimum(pos + fwd, seq_len) - jnp.maximum(pos - back, 0)).astype(F32)
        pooled = sum_scr[gi] / cnt - p_ref[:, cols]
        yp = _dot(pooled.astype(BF16), wpm_ref[gi]) * ps_ref[:, cols]
        pooled_parts.append(yp.astype(BF16))
    y_pool = jnp.concatenate(pooled_parts, axis=1)
    a = _dot(yd_ref[...], wdo_ref[...])
    b = _dot(y_pool, wpo_ref[...])
    merged = gt_ref[:, :D_MODEL] * a + gt_ref[:, D_MODEL:] * b
    mix = _dot(merged.astype(BF16), wo_ref[...])
    o_ref[...] = h_ref[...] + _rms(mix, g_ref[...])


def _mix_out(h, y_delta, gates, pool_in, seq_len, w_delta_out, w_pool_mix, pool_scale, w_pool_out,
             w_o, post_g):
    t = h.shape[0]
    tm = TOKEN_TILE
    tps = seq_len // tm
    halo_per_tile = tm // POOL_HALO
    n_halo_blocks = t // POOL_HALO

    def row(n):
        return pl.BlockSpec((tm, n), lambda i: (i, 0))

    prev = pl.BlockSpec((POOL_HALO, D_POOL), lambda i: (jnp.maximum(i * halo_per_tile - 1, 0), 0))
    nxt = pl.BlockSpec((POOL_HALO, D_POOL),
                       lambda i: (jnp.minimum((i + 1) * halo_per_tile, n_halo_blocks - 1), 0))
    return pl.pallas_call(
        functools.partial(_mix_out_body, tm=tm, tiles_per_seq=tps, seq_len=seq_len),
        grid=(t // tm,),
        in_specs=[row(D_MODEL), row(D_MODEL), row(2 * D_MODEL), prev, row(D_POOL), nxt,
                  _resident((D_MODEL, D_MODEL)),
                  _resident((len(POOL_WINDOWS), POOL_GROUP_DIM, POOL_GROUP_DIM)),
                  _resident((1, D_POOL)), _resident((D_POOL, D_MODEL)),
                  _resident((D_MODEL, D_MODEL)), _resident((1, D_MODEL))],
        out_specs=row(D_MODEL),
        out_shape=jax.ShapeDtypeStruct((t, D_MODEL), F32),
        scratch_shapes=[pltpu.VMEM((len(POOL_WINDOWS), tm + 2 * POOL_HALO, POOL_GROUP_DIM), F32),
                        pltpu.VMEM((len(POOL_WINDOWS), tm, POOL_GROUP_DIM), F32)],
        compiler_params=_params(1),
        name="mix_out",
    )(h, y_delta, gates, pool_in, pool_in, pool_in, w_delta_out, w_pool_mix, pool_scale, w_pool_out,
      w_o, post_g)


def _layer(x, w):
    batch, seq_len, _ = x.shape
    assert seq_len % TOKEN_TILE == 0 and seq_len % (DELTA_CHUNKS_PER_STEP * DELTA_CHUNK) == 0
    h0 = x.reshape(batch * seq_len, D_MODEL)
    h1 = _ffn(h0, w["ffn1_pre_g"], w["ffn1_w_in"], w["ffn1_w_out"], w["ffn1_post_g"], w["final_g"],
              apply_final=False)
    q, k, kt, v, sz, gb, gct, pool_in, gates = _in_proj(
        h1, seq_len, w["mix_pre_g"], w["w_qkv"], w["conv_w"], w["w_z"], w["w_gb"], w["w_pool"],
        w["w_gates"], w["alog_row"], w["dtb_row"])
    o_fwd = _delta(q, k, kt, v, gb, gct, batch, seq_len)
    y_delta = _delta(q, k, kt, v, gb, gct, batch, seq_len, merge=(o_fwd, sz, w["delta_norm_g"]))
    h2 = _mix_out(h1, y_delta, gates, pool_in, seq_len, w["w_delta_out"], w["w_pool_mix"],
                  w["pool_scale"], w["w_pool_out"], w["w_o"], w["mix_post_g"])
    h3 = _ffn(h2, w["ffn2_pre_g"], w["ffn2_w_in"], w["ffn2_w_out"], w["ffn2_post_g"], w["final_g"],
              apply_final=True)
    return h3.reshape(batch, seq_len, D_MODEL)


def _prepare_weights(ffn1_pre_g, ffn1_w_in, ffn1_w_out, ffn1_post_g, mix_pre_g, w_in, conv_w, a_log,
                     dt_bias, delta_norm_g, w_delta_out, w_pool_mix, pool_scale, w_pool_out, w_o,
                     mix_post_g, ffn2_pre_g, ffn2_w_in, ffn2_w_out, ffn2_post_g, final_g):
    def gain(g):
        return g[0].reshape(1, -1).astype(F32)

    def mxu(m):
        return m.astype(BF16)

    n_dh = N_DIRS * N_HEADS
    o_z = 3 * D_MODEL
    o_dec = o_z + D_MODEL
    o_pool = o_dec + 2 * n_dh
    o_gates = o_pool + D_POOL
    wi = w_in[0]
    lane_pad = LANES - 2 * n_dh
    w_gb = jnp.pad(wi[:, o_dec:o_pool], ((0, 0), (0, lane_pad)))
    alog_row = jnp.pad(a_log[0].reshape(1, n_dh), ((0, 0), (0, LANES - n_dh))).astype(F32)
    dtb_row = jnp.pad(dt_bias[0].reshape(1, n_dh), ((0, 0), (0, LANES - n_dh))).astype(F32)
    return dict(
        ffn1_pre_g=gain(ffn1_pre_g), ffn1_w_in=mxu(ffn1_w_in[0]), ffn1_w_out=mxu(ffn1_w_out[0]),
        ffn1_post_g=gain(ffn1_post_g), mix_pre_g=gain(mix_pre_g),
        w_qkv=mxu(wi[:, :o_z]), w_z=mxu(wi[:, o_z:o_dec]), w_gb=mxu(w_gb),
        w_pool=mxu(wi[:, o_pool:o_gates]), w_gates=mxu(wi[:, o_gates:]),
        conv_w=conv_w[0].astype(F32), alog_row=alog_row, dtb_row=dtb_row,
        delta_norm_g=gain(delta_norm_g), w_delta_out=mxu(w_delta_out[0]),
        w_pool_mix=mxu(w_pool_mix[0]), pool_scale=pool_scale[0].reshape(1, D_POOL).astype(F32),
        w_pool_out=mxu(w_pool_out[0]), w_o=mxu(w_o[0]), mix_post_g=gain(mix_post_g),
        ffn2_pre_g=gain(ffn2_pre_g), ffn2_w_in=mxu(ffn2_w_in[0]), ffn2_w_out=mxu(ffn2_w_out[0]),
        ffn2_post_g=gain(ffn2_post_g), final_g=gain(final_g))


def kernel(x_prompt, x_sample, ffn1_pre_g, ffn1_w_in, ffn1_w_out, ffn1_post_g, mix_pre_g, w_in, conv_w,
           a_log, dt_bias, delta_norm_g, w_delta_out, w_pool_mix, pool_scale, w_pool_out, w_o,
           mix_post_g, ffn2_pre_g, ffn2_w_in, ffn2_w_out, ffn2_post_g, final_g):
    assert ffn1_w_in.shape[0] == 1, "single-layer trunk"
    w = _prepare_weights(ffn1_pre_g, ffn1_w_in, ffn1_w_out, ffn1_post_g, mix_pre_g, w_in, conv_w, a_log,
                         dt_bias, delta_norm_g, w_delta_out, w_pool_mix, pool_scale, w_pool_out, w_o,
                         mix_post_g, ffn2_pre_g, ffn2_w_in, ffn2_w_out, ffn2_post_g, final_g)
    return (_layer(x_prompt, w), _layer(x_sample, w))
```

```python
import functools

import jax
import jax.numpy as jnp
from jax import lax
from jax.experimental import pallas as pl
from jax.experimental.pallas import tpu as pltpu

F32 = jnp.float32
BF16 = jnp.bfloat16

D_MODEL = 1024
HEAD_DIM = 128
N_HEADS = D_MODEL // HEAD_DIM
N_DIRS = 2
CONV_WIDTH = 5
POOL_WINDOWS = (2, 4, 8, 16)
POOL_GROUP_DIM = 128
D_POOL = len(POOL_WINDOWS) * POOL_GROUP_DIM
D_FF = 2816
RMS_EPS = 1e-6
L2_EPS = 1e-6

LANES = 128
SUBLANES = 8
CONV_ROW_STRIDE = 4
POOL_ROW_STRIDE = 4
FF_CHUNK = 256
TOKEN_TILE = 512
WIDE_TOKEN_TILE = 1024
DELTA_CHUNK = 128
DELTA_CHUNKS_PER_STEP = 4
QKV_HALO = 16
POOL_HALO = 8
VMEM_LIMIT_BYTES = 56 * 1024 * 1024


def _rms(x, g):
    ms = jnp.mean(x * x, axis=-1, keepdims=True)
    return x * lax.rsqrt(ms + RMS_EPS) * g


def _sigmoid(x):
    return 0.5 * jnp.tanh(0.5 * x) + 0.5


def _silu_of_half(half):
    return half + half * jnp.tanh(half)


def _silu(x):
    return _silu_of_half(0.5 * x)


def _dot(a, b):
    return jnp.dot(a, b, preferred_element_type=F32)


def _resident(shape):
    return pl.BlockSpec(shape, lambda *_: (0,) * len(shape), pipeline_mode=pl.Buffered(1))


def _params(n_grid_dims):
    return pltpu.CompilerParams(
        dimension_semantics=("arbitrary",) * n_grid_dims,
        vmem_limit_bytes=VMEM_LIMIT_BYTES)


def _ffn_body(x_ref, pre_g_ref, w_in_ref, w_out_ref, post_g_ref, final_g_ref, o_ref, *, apply_final):
    x = x_ref[...]
    xn = _rms(x, pre_g_ref[...]).astype(BF16)
    acc = jnp.zeros(x.shape, F32)
    for c in range(D_FF // FF_CHUNK):
        lo = c * FF_CHUNK
        gate = _dot(xn, w_in_ref[:, lo:lo + FF_CHUNK])
        up = _dot(xn, w_in_ref[:, D_FF + lo:D_FF + lo + FF_CHUNK])
        act = (_silu(gate) * up).astype(BF16)
        acc = acc + _dot(act, w_out_ref[lo:lo + FF_CHUNK, :])
    h = x + 0.5 * _rms(acc, post_g_ref[...])
    if apply_final:
        h = _rms(h, final_g_ref[...])
    o_ref[...] = h


def _ffn(x, pre_g, w_in, w_out, post_g, final_g, *, apply_final):
    t = x.shape[0]
    tm = WIDE_TOKEN_TILE
    row = pl.BlockSpec((tm, D_MODEL), lambda i: (i, 0))
    return pl.pallas_call(
        functools.partial(_ffn_body, apply_final=apply_final),
        grid=(t // tm,),
        in_specs=[row, _resident((1, D_MODEL)), _resident((D_MODEL, 2 * D_FF)),
                  _resident((D_FF, D_MODEL)), _resident((1, D_MODEL)), _resident((1, D_MODEL))],
        out_specs=row,
        out_shape=jax.ShapeDtypeStruct((t, D_MODEL), F32),
        compiler_params=_params(1),
        name="ffn_final" if apply_final else "ffn",
    )(x, pre_g, w_in, w_out, post_g, final_g)


def _split3(x):
    a = x.astype(BF16)
    r = x - a.astype(F32)
    b = r.astype(BF16)
    c = (r - b.astype(F32)).astype(BF16)
    return a, b, c


def _in_proj_body(xp_ref, x_ref, xn_ref, g_ref, wqkv_ref, cw_ref, wz_ref, wgb_ref, wp_ref, wg_ref,
                  alog_ref, dtb_ref,
                  q_ref, k_ref, kt_ref, v_ref, sz_ref, gb_ref, gct_ref, p_ref, gt_ref,
                  xe_scr, raw_scr, *, tm, tiles_per_seq):
    t = lax.rem(pl.program_id(0), tiles_per_seq)
    g = g_ref[...]
    keep_prev = (t > 0).astype(F32)
    keep_next = (t < tiles_per_seq - 1).astype(F32)
    xe_scr[0:QKV_HALO, :] = (_rms(xp_ref[...], g) * keep_prev).astype(BF16)
    xe_scr[QKV_HALO:QKV_HALO + tm, :] = _rms(x_ref[...], g).astype(BF16)
    xe_scr[QKV_HALO + tm:, :] = (_rms(xn_ref[...], g) * keep_next).astype(BF16)
    xn = xe_scr[QKV_HALO:QKV_HALO + tm, :]

    def conv_section(s, out_ref):
        raw = _dot(xe_scr[...], wqkv_ref[:, s * D_MODEL:(s + 1) * D_MODEL])
        for h in range(N_HEADS):
            raw_scr[h] = raw[:, h * HEAD_DIM:(h + 1) * HEAD_DIM]
        scale = HEAD_DIM ** -0.5 if s == 0 else 1.0
        pad = CONV_WIDTH // 2
        group = SUBLANES * CONV_ROW_STRIDE
        for h in range(N_HEADS):
            lanes = slice(s * D_MODEL + h * HEAD_DIM, s * D_MODEL + (h + 1) * HEAD_DIM)
            half_taps = [0.5 * cw_ref[i:i + 1, lanes] for i in range(CONV_WIDTH)]
            for first in range(0, tm, group):
                window = [raw_scr[h, pl.ds(QKV_HALO - pad + first + j, SUBLANES, stride=CONV_ROW_STRIDE), :]
                          for j in range(CONV_ROW_STRIDE + CONV_WIDTH - 1)]
                for phase in range(CONV_ROW_STRIDE):
                    half = None
                    for i in range(CONV_WIDTH):
                        tap = window[phase + i] * half_taps[i]
                        half = tap if half is None else half + tap
                    y = _silu_of_half(half)
                    if s < 2:
                        ss = jnp.sum(y * y, axis=-1, keepdims=True)
                        y = y * (lax.rsqrt(ss + L2_EPS) * scale)
                    out_ref[h, pl.ds(first + phase, SUBLANES, stride=CONV_ROW_STRIDE), :] = y
        if s == 1:
            for h in range(N_HEADS):
                kt_ref[h * HEAD_DIM:(h + 1) * HEAD_DIM, :] = out_ref[h].T

    def gates_columns(lo, hi):
        gt_ref[:, lo:hi] = _sigmoid(_dot(xn, wg_ref[:, lo:hi])).astype(gt_ref.dtype)

    def decay_and_beta():
        raw = _dot(xn, wgb_ref[...])
        sp_in = raw + dtb_ref[...]
        softplus = jnp.maximum(sp_in, 0.0) + jnp.log(1.0 + jnp.exp(-jnp.abs(sp_in)))
        log_decay = -jnp.exp(alog_ref[...]) * softplus
        beta = _sigmoid(raw)
        c = DELTA_CHUNK
        row = lax.broadcasted_iota(jnp.int32, (c, c), 0)
        col = lax.broadcasted_iota(jnp.int32, (c, c), 1)
        ones_lower = (row >= col).astype(BF16)
        ones_upper = (row <= col).astype(BF16)
        lane = lax.broadcasted_iota(jnp.int32, (c, LANES), 1)
        for j in range(tm // c):
            rows = slice(j * c, (j + 1) * c)
            g_fwd = jnp.where(lane < N_HEADS, log_decay[rows], 0.0)
            g_bwd = jnp.where((lane >= N_HEADS) & (lane < N_DIRS * N_HEADS), log_decay[rows], 0.0)
            gc = (sum(_dot(ones_lower, part) for part in _split3(g_fwd))
                  + sum(_dot(ones_upper, part) for part in _split3(g_bwd)))
            gb_ref[rows, :] = jnp.where(lane < N_DIRS * N_HEADS, gc, beta[rows])
            gct_ref[:, rows] = gc.T

    conv_section(0, q_ref)
    sz_ref[...] = _silu_of_half(0.5 * _dot(xn, wz_ref[...]))
    conv_section(1, k_ref)
    gates_columns(0, D_MODEL)
    conv_section(2, v_ref)
    gates_columns(D_MODEL, 2 * D_MODEL)
    p_ref[...] = _dot(xn, wp_ref[...])
    decay_and_beta()


def _in_proj(h, seq_len, g, w_qkv, conv_w, w_z, w_gb, w_pool, w_gates, alog_row, dtb_row):
    t = h.shape[0]
    tm = TOKEN_TILE
    tps = seq_len // tm
    halo_per_tile = tm // QKV_HALO
    n_halo_blocks = t // QKV_HALO

    def row(n):
        return pl.BlockSpec((tm, n), lambda i: (i, 0))

    def col(n):
        return pl.BlockSpec((n, tm), lambda i: (0, i))

    def out(n, dtype=F32):
        return jax.ShapeDtypeStruct((t, n), dtype)

    def out_t(n):
        return jax.ShapeDtypeStruct((n, t), F32)

    heads_spec = pl.BlockSpec((N_HEADS, tm, HEAD_DIM), lambda i: (0, i, 0))
    heads_shape = jax.ShapeDtypeStruct((N_HEADS, t, HEAD_DIM), F32)

    prev = pl.BlockSpec((QKV_HALO, D_MODEL), lambda i: (jnp.maximum(i * halo_per_tile - 1, 0), 0))
    nxt = pl.BlockSpec((QKV_HALO, D_MODEL),
                       lambda i: (jnp.minimum((i + 1) * halo_per_tile, n_halo_blocks - 1), 0))
    return pl.pallas_call(
        functools.partial(_in_proj_body, tm=tm, tiles_per_seq=tps),
        grid=(t // tm,),
        in_specs=[prev, row(D_MODEL), nxt, _resident((1, D_MODEL)), _resident((D_MODEL, 3 * D_MODEL)),
                  _resident((CONV_WIDTH, 3 * D_MODEL)), _resident((D_MODEL, D_MODEL)),
                  _resident((D_MODEL, LANES)), _resident((D_MODEL, D_POOL)),
                  _resident((D_MODEL, 2 * D_MODEL)), _resident((1, LANES)), _resident((1, LANES))],
        out_specs=[heads_spec, heads_spec, col(D_MODEL), heads_spec, row(D_MODEL), row(LANES),
                   col(LANES), row(D_POOL), row(2 * D_MODEL)],
        out_shape=[heads_shape, heads_shape, out_t(D_MODEL), heads_shape, out(D_MODEL), out(LANES),
                   out_t(LANES), out(D_POOL), out(2 * D_MODEL, BF16)],
        scratch_shapes=[pltpu.VMEM((tm + 2 * QKV_HALO, D_MODEL), BF16),
                        pltpu.VMEM((N_HEADS, tm + 2 * QKV_HALO, HEAD_DIM), F32)],
        compiler_params=_params(1),
        name="in_proj",
    )(h, h, h, g, w_qkv, conv_w, w_z, w_gb, w_pool, w_gates, alog_row, dtb_row)


def _delta_body(q_ref, k_ref, kt_ref, v_ref, gb_ref, gct_ref, *rest, reverse, n_chunks):
    if reverse:
        ofwd_ref, sz_ref, ng_ref, out_ref, s_scr = rest
    else:
        out_ref, s_scr = rest
    c = DELTA_CHUNK

    @pl.when(pl.program_id(1) == 0)
    def _():
        s_scr[...] = jnp.zeros(s_scr.shape, F32)

    row = lax.broadcasted_iota(jnp.int32, (c, c), 0)
    col = lax.broadcasted_iota(jnp.int32, (c, c), 1)
    incl = (row <= col) if reverse else (row >= col)
    strict = (row < col) if reverse else (row > col)
    last = 0 if reverse else c - 1
    direction = 1 if reverse else 0
    n_steps = c.bit_length() - 2
    heads = range(N_HEADS)

    pairs = [(g, h) for g in range(n_chunks) for h in heads]
    loc = {}
    for g, h in pairs:
        rows = slice(g * c, (g + 1) * c)
        hs = slice(h * HEAD_DIM, (h + 1) * HEAD_DIM)
        cg = direction * N_HEADS + h
        cb = N_DIRS * N_HEADS + cg
        gc_c = gb_ref[rows, cg:cg + 1]
        gc_r = gct_ref[cg:cg + 1, rows]
        beta_c = gb_ref[rows, cb:cb + 1]
        decay = jnp.exp(jnp.where(incl, gc_c - gc_r, -jnp.inf))
        e_c = jnp.exp(gc_c)
        q = q_ref[h, rows, :]
        k = k_ref[h, rows, :]
        v = v_ref[h, rows, :]
        kt = kt_ref[hs, rows]
        kb = k * beta_c
        pr = _dot(jnp.concatenate([kb, q], axis=0).astype(BF16), kt.astype(BF16))
        loc[g, h] = dict(
            lmat=jnp.where(strict, pr[:c] * decay, 0.0), qk=(pr[c:] * decay).astype(BF16),
            rhs=jnp.concatenate([v * beta_c, kb * e_c], axis=1),
            qe=(q * e_c).astype(BF16),
            k_decay_t=(kt * decay[last:last + 1, :]).astype(BF16),
            state_decay=jnp.exp(gc_r[:, last:last + 1]))

    for d in loc.values():
        lb = d["lmat"].astype(BF16)
        d["m"] = _dot(lb, lb)
        d["r"] = -d.pop("lmat")
    for step in range(1, n_steps + 1):
        for d in loc.values():
            r, m = d["r"], d["m"]
            mb = m.astype(BF16)
            if step < n_steps:
                pr2 = _dot(jnp.concatenate([r, m], axis=0).astype(BF16), mb)
                d["r"] = r + m + pr2[:c]
                d["m"] = pr2[c:]
            else:
                d["r"] = r + m + _dot(r.astype(BF16), mb)
    for d in loc.values():
        rhs = d.pop("rhs")
        sol = rhs + _dot(d.pop("r").astype(BF16), rhs.astype(BF16))
        d["u"] = sol[:, :HEAD_DIM]
        d["w"] = sol[:, HEAD_DIM:].astype(BF16)

    states = [s_scr[h] for h in heads]
    for g in (reversed(range(n_chunks)) if reverse else range(n_chunks)):
        rows = slice(g * c, (g + 1) * c)
        pr3 = [_dot(jnp.concatenate([loc[g, h]["w"], loc[g, h]["qe"]], axis=0), states[h].astype(BF16))
               for h in heads]
        v_new = [(loc[g, h]["u"] - pr3[h][:c]).astype(BF16) for h in heads]
        pr4 = [_dot(jnp.concatenate([loc[g, h]["qk"], loc[g, h]["k_decay_t"]], axis=0), v_new[h])
               for h in heads]
        states = [states[h] * loc[g, h]["state_decay"] + pr4[h][c:] for h in heads]
        for h in heads:
            hs = slice(h * HEAD_DIM, (h + 1) * HEAD_DIM)
            o = pr3[h][c:] + pr4[h][:c]
            if reverse:
                tot = o + ofwd_ref[rows, hs]
                ms = jnp.mean(tot * tot, axis=-1, keepdims=True)
                y = tot * lax.rsqrt(ms + RMS_EPS) * ng_ref[...] * sz_ref[rows, hs]
                out_ref[rows, hs] = y.astype(out_ref.dtype)
            else:
                out_ref[rows, hs] = o
    for h in heads:
        s_scr[h] = states[h]


def _delta(q, k, kt, v, gb, gct, batch, seq_len, merge=None):
    reverse = merge is not None
    t = gb.shape[0]
    c = DELTA_CHUNKS_PER_STEP * DELTA_CHUNK
    n = seq_len // c

    def blk(b, j):
        return (b * n + (n - 1 - j if reverse else j), 0)

    def row(width):
        return pl.BlockSpec((c, width), blk)

    def col(height):
        return pl.BlockSpec((height, c), lambda b, j: blk(b, j)[::-1])

    heads = pl.BlockSpec((N_HEADS, c, HEAD_DIM), lambda b, j: (0,) + blk(b, j))
    in_specs = [heads, heads, col(D_MODEL), heads, row(LANES), col(LANES)]
    operands = [q, k, kt, v, gb, gct]
    if reverse:
        in_specs += [row(D_MODEL), row(D_MODEL), _resident((1, HEAD_DIM))]
        operands += list(merge)
    return pl.pallas_call(
        functools.partial(_delta_body, reverse=reverse, n_chunks=DELTA_CHUNKS_PER_STEP),
        grid=(batch, n),
        in_specs=in_specs,
        out_specs=row(D_MODEL),
        out_shape=jax.ShapeDtypeStruct((t, D_MODEL), BF16 if reverse else F32),
        scratch_shapes=[pltpu.VMEM((N_HEADS, HEAD_DIM, HEAD_DIM), F32)],
        compiler_params=_params(2),
        name="delta_bwd" if reverse else "delta_fwd",
    )(*operands)


def _mix_out_body(h_ref, yd_ref, gt_ref, pp_ref, p_ref, pn_ref, wdo_ref, wpm_ref, ps_ref, wpo_ref,
                  wo_ref, g_ref, o_ref, pad_scr, sum_scr, *, tm, tiles_per_seq, seq_len):
    t = lax.rem(pl.program_id(0), tiles_per_seq)
    keep_prev = (t > 0).astype(F32)
    keep_next = (t < tiles_per_seq - 1).astype(F32)
    for gi in range(len(POOL_WINDOWS)):
        cols = slice(gi * POOL_GROUP_DIM, (gi + 1) * POOL_GROUP_DIM)
        pad_scr[gi, 0:POOL_HALO, :] = pp_ref[:, cols] * keep_prev
        pad_scr[gi, POOL_HALO:POOL_HALO + tm, :] = p_ref[:, cols]
        pad_scr[gi, POOL_HALO + tm:, :] = pn_ref[:, cols] * keep_next
    pos = t * tm + lax.broadcasted_iota(jnp.int32, (tm, 1), 0)
    group = SUBLANES * POOL_ROW_STRIDE
    pooled_parts = []
    for gi, win in enumerate(POOL_WINDOWS):
        cols = slice(gi * POOL_GROUP_DIM, (gi + 1) * POOL_GROUP_DIM)
        back = win // 2
        fwd = win - back
        for first in range(0, tm, group):
            window = [pad_scr[gi, pl.ds(POOL_HALO - back + first + j, SUBLANES, stride=POOL_ROW_STRIDE), :]
                      for j in range(POOL_ROW_STRIDE + win - 1)]
            for phase in range(POOL_ROW_STRIDE):
                total = window[phase]
                for j in range(1, win):
                    total = total + window[phase + j]
                sum_scr[gi, pl.ds(first + phase, SUBLANES, stride=POOL_ROW_STRIDE), :] = total
        cnt = (jnp.minimum(pos + fwd, seq_len) - jnp.maximum(pos - back, 0)).astype(F32)
        pooled = sum_scr[gi] / cnt - p_ref[:, cols]
        yp = _dot(pooled.astype(BF16), wpm_ref[gi]) * ps_ref[:, cols]
        pooled_parts.append(yp.astype(BF16))
    y_pool = jnp.concatenate(pooled_parts, axis=1)
    a = _dot(yd_ref[...], wdo_ref[...])
    b = _dot(y_pool, wpo_ref[...])
    merged = gt_ref[:, :D_MODEL] * a + gt_ref[:, D_MODEL:] * b
    mix = _dot(merged.astype(BF16), wo_ref[...])
    o_ref[...] = h_ref[...] + _rms(mix, g_ref[...])


def _mix_out(h, y_delta, gates, pool_in, seq_len, w_delta_out, w_pool_mix, pool_scale, w_pool_out,
             w_o, post_g):
    t = h.shape[0]
    tm = WIDE_TOKEN_TILE
    tps = seq_len // tm
    halo_per_tile = tm // POOL_HALO
    n_halo_blocks = t // POOL_HALO

    def row(n):
        return pl.BlockSpec((tm, n), lambda i: (i, 0))

    prev = pl.BlockSpec((POOL_HALO, D_POOL), lambda i: (jnp.maximum(i * halo_per_tile - 1, 0), 0))
    nxt = pl.BlockSpec((POOL_HALO, D_POOL),
                       lambda i: (jnp.minimum((i + 1) * halo_per_tile, n_halo_blocks - 1), 0))
    return pl.pallas_call(
        functools.partial(_mix_out_body, tm=tm, tiles_per_seq=tps, seq_len=seq_len),
        grid=(t // tm,),
        in_specs=[row(D_MODEL), row(D_MODEL), row(2 * D_MODEL), prev, row(D_POOL), nxt,
                  _resident((D_MODEL, D_MODEL)),
                  _resident((len(POOL_WINDOWS), POOL_GROUP_DIM, POOL_GROUP_DIM)),
                  _resident((1, D_POOL)), _resident((D_POOL, D_MODEL)),
                  _resident((D_MODEL, D_MODEL)), _resident((1, D_MODEL))],
        out_specs=row(D_MODEL),
        out_shape=jax.ShapeDtypeStruct((t, D_MODEL), F32),
        scratch_shapes=[pltpu.VMEM((len(POOL_WINDOWS), tm + 2 * POOL_HALO, POOL_GROUP_DIM), F32),
                        pltpu.VMEM((len(POOL_WINDOWS), tm, POOL_GROUP_DIM), F32)],
        compiler_params=_params(1),
        name="mix_out",
    )(h, y_delta, gates, pool_in, pool_in, pool_in, w_delta_out, w_pool_mix, pool_scale, w_pool_out,
      w_o, post_g)


def _layer(x, w):
    batch, seq_len, _ = x.shape
    assert seq_len % WIDE_TOKEN_TILE == 0 and seq_len % (DELTA_CHUNKS_PER_STEP * DELTA_CHUNK) == 0
    h0 = x.reshape(batch * seq_len, D_MODEL)
    h1 = _ffn(h0, w["ffn1_pre_g"], w["ffn1_w_in"], w["ffn1_w_out"], w["ffn1_post_g"], w["final_g"],
              apply_final=False)
    q, k, kt, v, sz, gb, gct, pool_in, gates = _in_proj(
        h1, seq_len, w["mix_pre_g"], w["w_qkv"], w["conv_w"], w["w_z"], w["w_gb"], w["w_pool"],
        w["w_gates"], w["alog_row"], w["dtb_row"])
    o_fwd = _delta(q, k, kt, v, gb, gct, batch, seq_len)
    y_delta = _delta(q, k, kt, v, gb, gct, batch, seq_len, merge=(o_fwd, sz, w["delta_norm_g"]))
    h2 = _mix_out(h1, y_delta, gates, pool_in, seq_len, w["w_delta_out"], w["w_pool_mix"],
                  w["pool_scale"], w["w_pool_out"], w["w_o"], w["mix_post_g"])
    h3 = _ffn(h2, w["ffn2_pre_g"], w["ffn2_w_in"], w["ffn2_w_out"], w["ffn2_post_g"], w["final_g"],
              apply_final=True)
    return h3.reshape(batch, seq_len, D_MODEL)


def _prepare_weights(ffn1_pre_g, ffn1_w_in, ffn1_w_out, ffn1_post_g, mix_pre_g, w_in, conv_w, a_log,
                     dt_bias, delta_norm_g, w_delta_out, w_pool_mix, pool_scale, w_pool_out, w_o,
                     mix_post_g, ffn2_pre_g, ffn2_w_in, ffn2_w_out, ffn2_post_g, final_g):
    def gain(g):
        return g[0].reshape(1, -1).astype(F32)

    def mxu(m):
        return m.astype(BF16)

    n_dh = N_DIRS * N_HEADS
    o_z = 3 * D_MODEL
    o_dec = o_z + D_MODEL
    o_pool = o_dec + 2 * n_dh
    o_gates = o_pool + D_POOL
    wi = w_in[0]
    lane_pad = LANES - 2 * n_dh
    w_gb = jnp.pad(wi[:, o_dec:o_pool], ((0, 0), (0, lane_pad)))
    alog_row = jnp.pad(a_log[0].reshape(1, n_dh), ((0, 0), (0, LANES - n_dh))).astype(F32)
    dtb_row = jnp.pad(dt_bias[0].reshape(1, n_dh), ((0, 0), (0, LANES - n_dh))).astype(F32)
    return dict(
        ffn1_pre_g=gain(ffn1_pre_g), ffn1_w_in=mxu(ffn1_w_in[0]), ffn1_w_out=mxu(ffn1_w_out[0]),
        ffn1_post_g=gain(ffn1_post_g), mix_pre_g=gain(mix_pre_g),
        w_qkv=mxu(wi[:, :o_z]), w_z=mxu(wi[:, o_z:o_dec]), w_gb=mxu(w_gb),
        w_pool=mxu(wi[:, o_pool:o_gates]), w_gates=mxu(wi[:, o_gates:]),
        conv_w=conv_w[0].astype(F32), alog_row=alog_row, dtb_row=dtb_row,
        delta_norm_g=gain(delta_norm_g), w_delta_out=mxu(w_delta_out[0]),
        w_pool_mix=mxu(w_pool_mix[0]), pool_scale=pool_scale[0].reshape(1, D_POOL).astype(F32),
        w_pool_out=mxu(w_pool_out[0]), w_o=mxu(w_o[0]), mix_post_g=gain(mix_post_g),
        ffn2_pre_g=gain(ffn2_pre_g), ffn2_w_in=mxu(ffn2_w_in[0]), ffn2_w_out=mxu(ffn2_w_out[0]),
        ffn2_post_g=gain(ffn2_post_g), final_g=gain(final_g))


def kernel(x_prompt, x_sample, ffn1_pre_g, ffn1_w_in, ffn1_w_out, ffn1_post_g, mix_pre_g, w_in, conv_w,
           a_log, dt_bias, delta_norm_g, w_delta_out, w_pool_mix, pool_scale, w_pool_out, w_o,
           mix_post_g, ffn2_pre_g, ffn2_w_in, ffn2_w_out, ffn2_post_g, final_g):
    assert ffn1_w_in.shape[0] == 1, "single-layer trunk"
    w = _prepare_weights(ffn1_pre_g, ffn1_w_in, ffn1_w_out, ffn1_post_g, mix_pre_g, w_in, conv_w, a_log,
                         dt_bias, delta_norm_g, w_delta_out, w_pool_mix, pool_scale, w_pool_out, w_o,
                         mix_post_g, ffn2_pre_g, ffn2_w_in, ffn2_w_out, ffn2_post_g, final_g)
    return (_layer(x_prompt, w), _layer(x_sample, w))
```

```python
import functools

import jax
import jax.numpy as jnp
from jax import lax
from jax.experimental import pallas as pl
from jax.experimental.pallas import tpu as pltpu

F32 = jnp.float32
BF16 = jnp.bfloat16

D_MODEL = 1024
HEAD_DIM = 128
N_HEADS = D_MODEL // HEAD_DIM
N_DIRS = 2
CONV_WIDTH = 5
POOL_WINDOWS = (2, 4, 8, 16)
POOL_GROUP_DIM = 128
D_POOL = len(POOL_WINDOWS) * POOL_GROUP_DIM
D_FF = 2816
RMS_EPS = 1e-6
L2_EPS = 1e-6

LANES = 128
SUBLANES = 8
CONV_ROW_STRIDE = 4
POOL_ROW_STRIDE = 4
FF_CHUNK = 256
TOKEN_TILE = 512
WIDE_TOKEN_TILE = 1024
DELTA_CHUNK = 128
DELTA_CHUNKS_PER_STEP = 4
QKV_HALO = 16
POOL_HALO = 8
VMEM_LIMIT_BYTES = 56 * 1024 * 1024


def _rms(x, g):
    ms = jnp.mean(x * x, axis=-1, keepdims=True)
    return x * lax.rsqrt(ms + RMS_EPS) * g


def _sigmoid(x):
    return 0.5 * jnp.tanh(0.5 * x) + 0.5


def _silu_of_half(half):
    return half + half * jnp.tanh(half)


def _silu(x):
    return _silu_of_half(0.5 * x)


def _dot(a, b):
    return jnp.dot(a, b, preferred_element_type=F32)


def _resident(shape):
    return pl.BlockSpec(shape, lambda *_: (0,) * len(shape), pipeline_mode=pl.Buffered(1))


def _params(n_grid_dims):
    return pltpu.CompilerParams(
        dimension_semantics=("arbitrary",) * n_grid_dims,
        vmem_limit_bytes=VMEM_LIMIT_BYTES)


def _ffn_body(x_ref, pre_g_ref, w_in_ref, w_out_ref, post_g_ref, final_g_ref, o_ref, *, apply_final):
    x = x_ref[...]
    xn = _rms(x, pre_g_ref[...]).astype(BF16)
    acc = jnp.zeros(x.shape, F32)
    for c in range(D_FF // FF_CHUNK):
        lo = c * FF_CHUNK
        gate = _dot(xn, w_in_ref[:, lo:lo + FF_CHUNK])
        up = _dot(xn, w_in_ref[:, D_FF + lo:D_FF + lo + FF_CHUNK])
        act = (_silu(gate) * up).astype(BF16)
        acc = acc + _dot(act, w_out_ref[lo:lo + FF_CHUNK, :])
    h = x + 0.5 * _rms(acc, post_g_ref[...])
    if apply_final:
        h = _rms(h, final_g_ref[...])
    o_ref[...] = h


def _ffn(x, pre_g, w_in, w_out, post_g, final_g, *, apply_final):
    t = x.shape[0]
    tm = WIDE_TOKEN_TILE
    row = pl.BlockSpec((tm, D_MODEL), lambda i: (i, 0))
    return pl.pallas_call(
        functools.partial(_ffn_body, apply_final=apply_final),
        grid=(t // tm,),
        in_specs=[row, _resident((1, D_MODEL)), _resident((D_MODEL, 2 * D_FF)),
                  _resident((D_FF, D_MODEL)), _resident((1, D_MODEL)), _resident((1, D_MODEL))],
        out_specs=row,
        out_shape=jax.ShapeDtypeStruct((t, D_MODEL), F32),
        compiler_params=_params(1),
        name="ffn_final" if apply_final else "ffn",
    )(x, pre_g, w_in, w_out, post_g, final_g)


def _split3(x):
    a = x.astype(BF16)
    r = x - a.astype(F32)
    b = r.astype(BF16)
    c = (r - b.astype(F32)).astype(BF16)
    return a, b, c


def _in_proj_body(xp_ref, x_ref, xn_ref, g_ref, wqkv_ref, cw_ref, wz_ref, wgb_ref, wp_ref, wg_ref,
                  alog_ref, dtb_ref,
                  q_ref, k_ref, kt_ref, v_ref, sz_ref, gb_ref, gct_ref, p_ref, gt_ref,
                  xe_scr, raw_scr, *, tm, tiles_per_seq):
    t = lax.rem(pl.program_id(0), tiles_per_seq)
    g = g_ref[...]
    keep_prev = (t > 0).astype(F32)
    keep_next = (t < tiles_per_seq - 1).astype(F32)
    xe_scr[0:QKV_HALO, :] = (_rms(xp_ref[...], g) * keep_prev).astype(BF16)
    xe_scr[QKV_HALO:QKV_HALO + tm, :] = _rms(x_ref[...], g).astype(BF16)
    xe_scr[QKV_HALO + tm:, :] = (_rms(xn_ref[...], g) * keep_next).astype(BF16)
    xn = xe_scr[QKV_HALO:QKV_HALO + tm, :]

    def conv_section(s, out_ref):
        raw = _dot(xe_scr[...], wqkv_ref[:, s * D_MODEL:(s + 1) * D_MODEL])
        for h in range(N_HEADS):
            raw_scr[h] = raw[:, h * HEAD_DIM:(h + 1) * HEAD_DIM]
        scale = HEAD_DIM ** -0.5 if s == 0 else 1.0
        pad = CONV_WIDTH // 2
        group = SUBLANES * CONV_ROW_STRIDE
        for h in range(N_HEADS):
            lanes = slice(s * D_MODEL + h * HEAD_DIM, s * D_MODEL + (h + 1) * HEAD_DIM)
            half_taps = [0.5 * cw_ref[i:i + 1, lanes] for i in range(CONV_WIDTH)]
            for first in range(0, tm, group):
                window = [raw_scr[h, pl.ds(QKV_HALO - pad + first + j, SUBLANES, stride=CONV_ROW_STRIDE), :]
                          for j in range(CONV_ROW_STRIDE + CONV_WIDTH - 1)]
                for phase in range(CONV_ROW_STRIDE):
                    half = None
                    for i in range(CONV_WIDTH):
                        tap = window[phase + i] * half_taps[i]
                        half = tap if half is None else half + tap
                    y = _silu_of_half(half)
                    if s < 2:
                        ss = jnp.sum(y * y, axis=-1, keepdims=True)
                        y = y * (lax.rsqrt(ss + L2_EPS) * scale)
                    out_ref[h, pl.ds(first + phase, SUBLANES, stride=CONV_ROW_STRIDE), :] = y
        if s == 1:
            for h in range(N_HEADS):
                kt_ref[h * HEAD_DIM:(h + 1) * HEAD_DIM, :] = out_ref[h].T

    def gates_columns(lo, hi):
        gt_ref[:, lo:hi] = _sigmoid(_dot(xn, wg_ref[:, lo:hi])).astype(gt_ref.dtype)

    def decay_and_beta():
        raw = _dot(xn, wgb_ref[...])
        sp_in = raw + dtb_ref[...]
        softplus = jnp.maximum(sp_in, 0.0) + jnp.log(1.0 + jnp.exp(-jnp.abs(sp_in)))
        log_decay = -jnp.exp(alog_ref[...]) * softplus
        beta = _sigmoid(raw)
        c = DELTA_CHUNK
        row = lax.broadcasted_iota(jnp.int32, (c, c), 0)
        col = lax.broadcasted_iota(jnp.int32, (c, c), 1)
        ones_lower = (row >= col).astype(BF16)
        ones_upper = (row <= col).astype(BF16)
        lane = lax.broadcasted_iota(jnp.int32, (c, LANES), 1)
        for j in range(tm // c):
            rows = slice(j * c, (j + 1) * c)
            g_fwd = jnp.where(lane < N_HEADS, log_decay[rows], 0.0)
            g_bwd = jnp.where((lane >= N_HEADS) & (lane < N_DIRS * N_HEADS), log_decay[rows], 0.0)
            gc = (sum(_dot(ones_lower, part) for part in _split3(g_fwd))
                  + sum(_dot(ones_upper, part) for part in _split3(g_bwd)))
            gb_ref[rows, :] = jnp.where(lane < N_DIRS * N_HEADS, gc, beta[rows])
            gct_ref[:, rows] = gc.T

    conv_section(0, q_ref)
    sz_ref[...] = _silu_of_half(0.5 * _dot(xn, wz_ref[...]))
    conv_section(1, k_ref)
    gates_columns(0, D_MODEL)
    conv_section(2, v_ref)
    gates_columns(D_MODEL, 2 * D_MODEL)
    p_ref[...] = _dot(xn, wp_ref[...])
    decay_and_beta()


def _in_proj(h, seq_len, g, w_qkv, conv_w, w_z, w_gb, w_pool, w_gates, alog_row, dtb_row):
    t = h.shape[0]
    tm = TOKEN_TILE
    tps = seq_len // tm
    halo_per_tile = tm // QKV_HALO
    n_halo_blocks = t // QKV_HALO

    def row(n):
        return pl.BlockSpec((tm, n), lambda i: (i, 0))

    def col(n):
        return pl.BlockSpec((n, tm), lambda i: (0, i))

    def out(n, dtype=F32):
        return jax.ShapeDtypeStruct((t, n), dtype)

    def out_t(n):
        return jax.ShapeDtypeStruct((n, t), F32)

    heads_spec = pl.BlockSpec((N_HEADS, tm, HEAD_DIM), lambda i: (0, i, 0))
    heads_shape = jax.ShapeDtypeStruct((N_HEADS, t, HEAD_DIM), F32)

    prev = pl.BlockSpec((QKV_HALO, D_MODEL), lambda i: (jnp.maximum(i * halo_per_tile - 1, 0), 0))
    nxt = pl.BlockSpec((QKV_HALO, D_MODEL),
                       lambda i: (jnp.minimum((i + 1) * halo_per_tile, n_halo_blocks - 1), 0))
    return pl.pallas_call(
        functools.partial(_in_proj_body, tm=tm, tiles_per_seq=tps),
        grid=(t // tm,),
        in_specs=[prev, row(D_MODEL), nxt, _resident((1, D_MODEL)), _resident((D_MODEL, 3 * D_MODEL)),
                  _resident((CONV_WIDTH, 3 * D_MODEL)), _resident((D_MODEL, D_MODEL)),
                  _resident((D_MODEL, LANES)), _resident((D_MODEL, D_POOL)),
                  _resident((D_MODEL, 2 * D_MODEL)), _resident((1, LANES)), _resident((1, LANES))],
        out_specs=[heads_spec, heads_spec, col(D_MODEL), heads_spec, row(D_MODEL), row(LANES),
                   col(LANES), row(D_POOL), row(2 * D_MODEL)],
        out_shape=[heads_shape, heads_shape, out_t(D_MODEL), heads_shape, out(D_MODEL), out(LANES),
                   out_t(LANES), out(D_POOL), out(2 * D_MODEL, BF16)],
        scratch_shapes=[pltpu.VMEM((tm + 2 * QKV_HALO, D_MODEL), BF16),
                        pltpu.VMEM((N_HEADS, tm + 2 * QKV_HALO, HEAD_DIM), F32)],
        compiler_params=_params(1),
        name="in_proj",
    )(h, h, h, g, w_qkv, conv_w, w_z, w_gb, w_pool, w_gates, alog_row, dtb_row)


def _delta_body(q_ref, k_ref, kt_ref, v_ref, gb_ref, gct_ref, *rest, reverse, n_chunks):
    if reverse:
        ofwd_ref, sz_ref, ng_ref, out_ref, s_scr = rest
    else:
        out_ref, s_scr = rest
    c = DELTA_CHUNK

    @pl.when(pl.program_id(1) == 0)
    def _():
        s_scr[...] = jnp.zeros(s_scr.shape, F32)

    row = lax.broadcasted_iota(jnp.int32, (c, c), 0)
    col = lax.broadcasted_iota(jnp.int32, (c, c), 1)
    incl = (row <= col) if reverse else (row >= col)
    strict = (row < col) if reverse else (row > col)
    last = 0 if reverse else c - 1
    direction = 1 if reverse else 0
    n_steps = c.bit_length() - 2
    heads = range(N_HEADS)

    scan_chunks = list(reversed(range(n_chunks))) if reverse else list(range(n_chunks))
    all_loc = {}
    for wave in (scan_chunks[:n_chunks // 2], scan_chunks[n_chunks // 2:]):
        pairs = [(g, h) for g in wave for h in heads]
        loc = {}
        for g, h in pairs:
            rows = slice(g * c, (g + 1) * c)
            hs = slice(h * HEAD_DIM, (h + 1) * HEAD_DIM)
            cg = direction * N_HEADS + h
            cb = N_DIRS * N_HEADS + cg
            gc_c = gb_ref[rows, cg:cg + 1]
            gc_r = gct_ref[cg:cg + 1, rows]
            beta_c = gb_ref[rows, cb:cb + 1]
            decay = jnp.exp(jnp.where(incl, gc_c - gc_r, -jnp.inf))
            e_c = jnp.exp(gc_c)
            q = q_ref[h, rows, :]
            k = k_ref[h, rows, :]
            v = v_ref[h, rows, :]
            kt = kt_ref[hs, rows]
            kb = k * beta_c
            pr = _dot(jnp.concatenate([kb, q], axis=0).astype(BF16), kt.astype(BF16))
            loc[g, h] = dict(
                lmat=jnp.where(strict, pr[:c] * decay, 0.0), qk=(pr[c:] * decay).astype(BF16),
                rhs=jnp.concatenate([v * beta_c, kb * e_c], axis=1),
                qe=(q * e_c).astype(BF16),
                k_decay_t=(kt * decay[last:last + 1, :]).astype(BF16),
                state_decay=jnp.exp(gc_r[:, last:last + 1]))

        for d in loc.values():
            lb = d["lmat"].astype(BF16)
            d["m"] = _dot(lb, lb)
            d["r"] = -d.pop("lmat")
        for step in range(1, n_steps + 1):
            for d in loc.values():
                r, m = d["r"], d["m"]
                mb = m.astype(BF16)
                if step < n_steps:
                    pr2 = _dot(jnp.concatenate([r, m], axis=0).astype(BF16), mb)
                    d["r"] = r + m + pr2[:c]
                    d["m"] = pr2[c:]
                else:
                    d["r"] = r + m + _dot(r.astype(BF16), mb)
        for d in loc.values():
            rhs = d.pop("rhs")
            sol = rhs + _dot(d.pop("r").astype(BF16), rhs.astype(BF16))
            d["u"] = sol[:, :HEAD_DIM]
            d["w"] = sol[:, HEAD_DIM:].astype(BF16)
        all_loc.update(loc)
    loc = all_loc

    states = [s_scr[h] for h in heads]
    for g in (reversed(range(n_chunks)) if reverse else range(n_chunks)):
        rows = slice(g * c, (g + 1) * c)
        pr3 = [_dot(jnp.concatenate([loc[g, h]["w"], loc[g, h]["qe"]], axis=0), states[h].astype(BF16))
               for h in heads]
        v_new = [(loc[g, h]["u"] - pr3[h][:c]).astype(BF16) for h in heads]
        pr4 = [_dot(jnp.concatenate([loc[g, h]["qk"], loc[g, h]["k_decay_t"]], axis=0), v_new[h])
               for h in heads]
        states = [states[h] * loc[g, h]["state_decay"] + pr4[h][c:] for h in heads]
        for h in heads:
            hs = slice(h * HEAD_DIM, (h + 1) * HEAD_DIM)
            o = pr3[h][c:] + pr4[h][:c]
            if reverse:
                tot = o + ofwd_ref[rows, hs]
                ms = jnp.mean(tot * tot, axis=-1, keepdims=True)
                y = tot * lax.rsqrt(ms + RMS_EPS) * ng_ref[...] * sz_ref[rows, hs]
                out_ref[rows, hs] = y.astype(out_ref.dtype)
            else:
                out_ref[rows, hs] = o
    for h in heads:
        s_scr[h] = states[h]


def _delta(q, k, kt, v, gb, gct, batch, seq_len, merge=None):
    reverse = merge is not None
    t = gb.shape[0]
    c = DELTA_CHUNKS_PER_STEP * DELTA_CHUNK
    n = seq_len // c

    def blk(b, j):
        return (b * n + (n - 1 - j if reverse else j), 0)

    def row(width):
        return pl.BlockSpec((c, width), blk)

    def col(height):
        return pl.BlockSpec((height, c), lambda b, j: blk(b, j)[::-1])

    heads = pl.BlockSpec((N_HEADS, c, HEAD_DIM), lambda b, j: (0,) + blk(b, j))
    in_specs = [heads, heads, col(D_MODEL), heads, row(LANES), col(LANES)]
    operands = [q, k, kt, v, gb, gct]
    if reverse:
        in_specs += [row(D_MODEL), row(D_MODEL), _resident((1, HEAD_DIM))]
        operands += list(merge)
    return pl.pallas_call(
        functools.partial(_delta_body, reverse=reverse, n_chunks=DELTA_CHUNKS_PER_STEP),
        grid=(batch, n),
        in_specs=in_specs,
        out_specs=row(D_MODEL),
        out_shape=jax.ShapeDtypeStruct((t, D_MODEL), BF16 if reverse else F32),
        scratch_shapes=[pltpu.VMEM((N_HEADS, HEAD_DIM, HEAD_DIM), F32)],
        compiler_params=_params(2),
        name="delta_bwd" if reverse else "delta_fwd",
    )(*operands)


def _mix_out_body(h_ref, yd_ref, gt_ref, pp_ref, p_ref, pn_ref, wdo_ref, wpm_ref, ps_ref, wpo_ref,
                  wo_ref, g_ref, o_ref, pad_scr, sum_scr, *, tm, tiles_per_seq, seq_len):
    t = lax.rem(pl.program_id(0), tiles_per_seq)
    keep_prev = (t > 0).astype(F32)
    keep_next = (t < tiles_per_seq - 1).astype(F32)
    for gi in range(len(POOL_WINDOWS)):
        cols = slice(gi * POOL_GROUP_DIM, (gi + 1) * POOL_GROUP_DIM)
        pad_scr[gi, 0:POOL_HALO, :] = pp_ref[:, cols] * keep_prev
        pad_scr[gi, POOL_HALO:POOL_HALO + tm, :] = p_ref[:, cols]
        pad_scr[gi, POOL_HALO + tm:, :] = pn_ref[:, cols] * keep_next
    pos = t * tm + lax.broadcasted_iota(jnp.int32, (tm, 1), 0)
    group = SUBLANES * POOL_ROW_STRIDE
    pooled_parts = []
    for gi, win in enumerate(POOL_WINDOWS):
        cols = slice(gi * POOL_GROUP_DIM, (gi + 1) * POOL_GROUP_DIM)
        back = win // 2
        fwd = win - back
        for first in range(0, tm, group):
            window = [pad_scr[gi, pl.ds(POOL_HALO - back + first + j, SUBLANES, stride=POOL_ROW_STRIDE), :]
                      for j in range(POOL_ROW_STRIDE + win - 1)]
            for phase in range(POOL_ROW_STRIDE):
                total = window[phase]
                for j in range(1, win):
                    total = total + window[phase + j]
                sum_scr[gi, pl.ds(first + phase, SUBLANES, stride=POOL_ROW_STRIDE), :] = total
        cnt = (jnp.minimum(pos + fwd, seq_len) - jnp.maximum(pos - back, 0)).astype(F32)
        pooled = sum_scr[gi] / cnt - p_ref[:, cols]
        yp = _dot(pooled.astype(BF16), wpm_ref[gi]) * ps_ref[:, cols]
        pooled_parts.append(yp.astype(BF16))
    y_pool = jnp.concatenate(pooled_parts, axis=1)
    a = _dot(yd_ref[...], wdo_ref[...])
    b = _dot(y_pool, wpo_ref[...])
    merged = gt_ref[:, :D_MODEL] * a + gt_ref[:, D_MODEL:] * b
    mix = _dot(merged.astype(BF16), wo_ref[...])
    o_ref[...] = h_ref[...] + _rms(mix, g_ref[...])


def _mix_out(h, y_delta, gates, pool_in, seq_len, w_delta_out, w_pool_mix, pool_scale, w_pool_out,
             w_o, post_g):
    t = h.shape[0]
    tm = WIDE_TOKEN_TILE
    tps = seq_len // tm
    halo_per_tile = tm // POOL_HALO
    n_halo_blocks = t // POOL_HALO

    def row(n):
        return pl.BlockSpec((tm, n), lambda i: (i, 0))

    prev = pl.BlockSpec((POOL_HALO, D_POOL), lambda i: (jnp.maximum(i * halo_per_tile - 1, 0), 0))
    nxt = pl.BlockSpec((POOL_HALO, D_POOL),
                       lambda i: (jnp.minimum((i + 1) * halo_per_tile, n_halo_blocks - 1), 0))
    return pl.pallas_call(
        functools.partial(_mix_out_body, tm=tm, tiles_per_seq=tps, seq_len=seq_len),
        grid=(t // tm,),
        in_specs=[row(D_MODEL), row(D_MODEL), row(2 * D_MODEL), prev, row(D_POOL), nxt,
                  _resident((D_MODEL, D_MODEL)),
                  _resident((len(POOL_WINDOWS), POOL_GROUP_DIM, POOL_GROUP_DIM)),
                  _resident((1, D_POOL)), _resident((D_POOL, D_MODEL)),
                  _resident((D_MODEL, D_MODEL)), _resident((1, D_MODEL))],
        out_specs=row(D_MODEL),
        out_shape=jax.ShapeDtypeStruct((t, D_MODEL), F32),
        scratch_shapes=[pltpu.VMEM((len(POOL_WINDOWS), tm + 2 * POOL_HALO, POOL_GROUP_DIM), F32),
                        pltpu.VMEM((len(POOL_WINDOWS), tm, POOL_GROUP_DIM), F32)],
        compiler_params=_params(1),
        name="mix_out",
    )(h, y_delta, gates, pool_in, pool_in, pool_in, w_delta_out, w_pool_mix, pool_scale, w_pool_out,
      w_o, post_g)


def _layer(x, w):
    batch, seq_len, _ = x.shape
    assert seq_len % WIDE_TOKEN_TILE == 0 and seq_len % (DELTA_CHUNKS_PER_STEP * DELTA_CHUNK) == 0
    h0 = x.reshape(batch * seq_len, D_MODEL)
    h1 = _ffn(h0, w["ffn1_pre_g"], w["ffn1_w_in"], w["ffn1_w_out"], w["ffn1_post_g"], w["final_g"],
              apply_final=False)
    q, k, kt, v, sz, gb, gct, pool_in, gates = _in_proj(
        h1, seq_len, w["mix_pre_g"], w["w_qkv"], w["conv_w"], w["w_z"], w["w_gb"], w["w_pool"],
        w["w_gates"], w["alog_row"], w["dtb_row"])
    o_fwd = _delta(q, k, kt, v, gb, gct, batch, seq_len)
    y_delta = _delta(q, k, kt, v, gb, gct, batch, seq_len, merge=(o_fwd, sz, w["delta_norm_g"]))
    h2 = _mix_out(h1, y_delta, gates, pool_in, seq_len, w["w_delta_out"], w["w_pool_mix"],
                  w["pool_scale"], w["w_pool_out"], w["w_o"], w["mix_post_g"])
    h3 = _ffn(h2, w["ffn2_pre_g"], w["ffn2_w_in"], w["ffn2_w_out"], w["ffn2_post_g"], w["final_g"],
              apply_final=True)
    return h3.reshape(batch, seq_len, D_MODEL)


def _prepare_weights(ffn1_pre_g, ffn1_w_in, ffn1_w_out, ffn1_post_g, mix_pre_g, w_in, conv_w, a_log,
                     dt_bias, delta_norm_g, w_delta_out, w_pool_mix, pool_scale, w_pool_out, w_o,
                     mix_post_g, ffn2_pre_g, ffn2_w_in, ffn2_w_out, ffn2_post_g, final_g):
    def gain(g):
        return g[0].reshape(1, -1).astype(F32)

    def mxu(m):
        return m.astype(BF16)

    n_dh = N_DIRS * N_HEADS
    o_z = 3 * D_MODEL
    o_dec = o_z + D_MODEL
    o_pool = o_dec + 2 * n_dh
    o_gates = o_pool + D_POOL
    wi = w_in[0]
    lane_pad = LANES - 2 * n_dh
    w_gb = jnp.pad(wi[:, o_dec:o_pool], ((0, 0), (0, lane_pad)))
    alog_row = jnp.pad(a_log[0].reshape(1, n_dh), ((0, 0), (0, LANES - n_dh))).astype(F32)
    dtb_row = jnp.pad(dt_bias[0].reshape(1, n_dh), ((0, 0), (0, LANES - n_dh))).astype(F32)
    return dict(
        ffn1_pre_g=gain(ffn1_pre_g), ffn1_w_in=mxu(ffn1_w_in[0]), ffn1_w_out=mxu(ffn1_w_out[0]),
        ffn1_post_g=gain(ffn1_post_g), mix_pre_g=gain(mix_pre_g),
        w_qkv=mxu(wi[:, :o_z]), w_z=mxu(wi[:, o_z:o_dec]), w_gb=mxu(w_gb),
        w_pool=mxu(wi[:, o_pool:o_gates]), w_gates=mxu(wi[:, o_gates:]),
        conv_w=conv_w[0].astype(F32), alog_row=alog_row, dtb_row=dtb_row,
        delta_norm_g=gain(delta_norm_g), w_delta_out=mxu(w_delta_out[0]),
        w_pool_mix=mxu(w_pool_mix[0]), pool_scale=pool_scale[0].reshape(1, D_POOL).astype(F32),
        w_pool_out=mxu(w_pool_out[0]), w_o=mxu(w_o[0]), mix_post_g=gain(mix_post_g),
        ffn2_pre_g=gain(ffn2_pre_g), ffn2_w_in=mxu(ffn2_w_in[0]), ffn2_w_out=mxu(ffn2_w_out[0]),
        ffn2_post_g=gain(ffn2_post_g), final_g=gain(final_g))


def kernel(x_prompt, x_sample, ffn1_pre_g, ffn1_w_in, ffn1_w_out, ffn1_post_g, mix_pre_g, w_in, conv_w,
           a_log, dt_bias, delta_norm_g, w_delta_out, w_pool_mix, pool_scale, w_pool_out, w_o,
           mix_post_g, ffn2_pre_g, ffn2_w_in, ffn2_w_out, ffn2_post_g, final_g):
    assert ffn1_w_in.shape[0] == 1, "single-layer trunk"
    w = _prepare_weights(ffn1_pre_g, ffn1_w_in, ffn1_w_out, ffn1_post_g, mix_pre_g, w_in, conv_w, a_log,
                         dt_bias, delta_norm_g, w_delta_out, w_pool_mix, pool_scale, w_pool_out, w_o,
                         mix_post_g, ffn2_pre_g, ffn2_w_in, ffn2_w_out, ffn2_post_g, final_g)
    return (_layer(x_prompt, w), _layer(x_sample, w))
```
